```python
import math
import jax, jax.numpy as jnp
from jax import lax
import numpy as np

D_MODEL = 4096
BATCH = 4
SEQ = 2048
DEPTH = 2
DEC_BATCH = 16
DEC_SEQ = 32
PAST_LEN = 1024

CHUNK = 64
D_INNER = 2 * D_MODEL
W_POOL = D_INNER // 4
POOL_WINDOWS = (2, 4, 8, 16)
N_POOL_GROUPS = len(POOL_WINDOWS)
POOL_GW = W_POOL // N_POOL_GROUPS
POOL_HIST = max(POOL_WINDOWS) - 1
W_SSD = D_INNER - W_POOL
SSD_HEAD_DIM = 64
SSD_HEADS = W_SSD // SSD_HEAD_DIM
SSD_GROUPS = 8
SSD_HPG = SSD_HEADS // SSD_GROUPS
SSD_STATE = 128
CONV_W = 4
CONV_DIM = W_SSD + 2 * SSD_GROUPS * SSD_STATE
IN_DIM = 2 * W_POOL + W_SSD + CONV_DIM + SSD_HEADS
ALPHA = (2 * DEPTH) ** 0.25
BETA = (8 * DEPTH) ** -0.25
LN_EPS = 1e-5
RMS_EPS = 1e-5

kernel_name = "pool_ssd_hymba_stream_step"


def layer_norm(x):
    x32 = x.astype(jnp.float32)
    mu = jnp.mean(x32, axis=-1, keepdims=True)
    var = jnp.mean(jnp.square(x32 - mu), axis=-1, keepdims=True)
    return (x32 - mu) * lax.rsqrt(var + LN_EPS)


def pool_mixer(u, prev, start_pos, w_pool, pool_scale):
    b, L, _ = u.shape
    full = jnp.concatenate([prev.astype(u.dtype), u], axis=1)
    new_prev = full[:, -POOL_HIST:]
    f32 = full.astype(jnp.float32).reshape(b, POOL_HIST + L, N_POOL_GROUPS, POOL_GW)
    cs = jnp.concatenate([jnp.zeros_like(f32[:, :1]), jnp.cumsum(f32, axis=1)], axis=1)
    end = cs[:, POOL_HIST + 1:]
    pos = start_pos + jnp.arange(L)
    means = []
    for g, w in enumerate(POOL_WINDOWS):
        s = end[:, :, g] - cs[:, POOL_HIST + 1 - w: POOL_HIST + 1 - w + L, g]
        cnt = jnp.minimum(pos + 1, w).astype(jnp.float32)
        means.append(s / cnt[None, :, None])
    mean = jnp.stack(means, axis=2)
    pooled = mean - f32[:, POOL_HIST:]
    out = jnp.einsum('blgc,gcd->blgd', pooled, w_pool.astype(jnp.float32)).reshape(b, L, W_POOL)
    return out * pool_scale.astype(jnp.float32), new_prev


def causal_conv(u, prev, w, bias):
    L = u.shape[1]
    full = jnp.concatenate([prev.astype(u.dtype), u], axis=1)
    new_prev = full[:, -(CONV_W - 1):]
    f32 = full.astype(jnp.float32)
    w32 = w.astype(jnp.float32)
    out = sum(f32[:, k:k + L] * w32[k] for k in range(CONV_W)) + bias.astype(jnp.float32)
    return jax.nn.silu(out), new_prev


def ssd_scan(xh, dt, A, Bm, Cm, h0):
    b, L = xh.shape[:2]
    nc = -(-L // CHUNK)
    pad = nc * CHUNK - L
    padt = lambda t: jnp.pad(t, [(0, 0), (0, pad)] + [(0, 0)] * (t.ndim - 2))
    a = padt(dt * A)
    xdt = padt(xh * dt[..., None])
    Bp, Cp = padt(Bm), padt(Cm)
    chunk = lambda t: t.reshape((b, nc, CHUNK) + t.shape[2:])
    a, xdt, Bc, Cc = chunk(a), chunk(xdt), chunk(Bp), chunk(Cp)
    a_cs = jnp.cumsum(a, axis=2)
    seg = a_cs[:, :, :, None] - a_cs[:, :, None, :]
    tril = jnp.tril(jnp.ones((CHUNK, CHUNK), dtype=bool))[None, None, :, :, None, None]
    Lmat = jnp.exp(jnp.where(tril, seg, -jnp.inf))
    CB = jnp.einsum('bclgn,bcsgn->bclsg', Cc, Bc)
    y_diag = jnp.einsum('bclsg,bclsgk,bcsgkp->bclgkp', CB, Lmat, xdt)
    decay = jnp.exp(a_cs[:, :, -1:] - a_cs)
    states = jnp.einsum('bcsgn,bcsgk,bcsgkp->bcgkpn', Bc, decay, xdt)
    chunk_decay = jnp.exp(a_cs[:, :, -1])

    def step(h, inp):
        dec, st = inp
        return dec[..., None, None] * h + st, h

    h_final, h_prev = lax.scan(step, h0.astype(jnp.float32),
                               (jnp.moveaxis(chunk_decay, 1, 0), jnp.moveaxis(states, 1, 0)))
    h_prev = jnp.moveaxis(h_prev, 0, 1)
    y_off = jnp.einsum('bclgn,bcgkpn,bclgk->bclgkp', Cc, h_prev, jnp.exp(a_cs))
    y = (y_diag + y_off).reshape((b, nc * CHUNK) + xh.shape[2:])[:, :L]
    return y, h_final


def trunk_layer(x, c, pool_prev, conv_prev, ssm_prev, start_pos,
                w_ada, b_ada, w_in, w_pool, pool_scale, conv_w, conv_b,
                dt_bias, a_log, d_skip, ssd_norm_w, w_out, ln_g, ln_b):
    b, L, _ = x.shape
    mod = (jnp.einsum('bd,de->be', jax.nn.silu(c), w_ada) + b_ada).astype(jnp.float32)
    shift, scale, gate = jnp.split(mod, 3, axis=-1)
    h = (layer_norm(x) * (1.0 + scale[:, None]) + shift[:, None]).astype(x.dtype)
    proj = jnp.einsum('bld,de->ble', h, w_in)
    u_pool, g_pool, z, xbc, dt_raw = jnp.split(
        proj, [W_POOL, 2 * W_POOL, 2 * W_POOL + W_SSD, 2 * W_POOL + W_SSD + CONV_DIM], axis=-1)
    pool_out, new_pool = pool_mixer(u_pool, pool_prev, start_pos, w_pool, pool_scale)
    pool_out = pool_out * jax.nn.silu(g_pool.astype(jnp.float32))
    xbc_c, new_conv = causal_conv(xbc, conv_prev, conv_w, conv_b)
    xs, Bm, Cm = jnp.split(xbc_c, [W_SSD, W_SSD + SSD_GROUPS * SSD_STATE], axis=-1)
    dt = jax.nn.softplus(dt_raw.astype(jnp.float32) + dt_bias.astype(jnp.float32))
    A = -jnp.exp(a_log.astype(jnp.float32))
    xh = xs.reshape(b, L, SSD_GROUPS, SSD_HPG, SSD_HEAD_DIM)
    y, new_ssm = ssd_scan(xh, dt.reshape(b, L, SSD_GROUPS, SSD_HPG), A.reshape(SSD_GROUPS, SSD_HPG),
                          Bm.reshape(b, L, SSD_GROUPS, SSD_STATE), Cm.reshape(b, L, SSD_GROUPS, SSD_STATE),
                          ssm_prev.reshape(b, SSD_GROUPS, SSD_HPG, SSD_HEAD_DIM, SSD_STATE))
    y = y + d_skip.astype(jnp.float32).reshape(SSD_GROUPS, SSD_HPG)[:, :, None] * xh
    y = y.reshape(b, L, W_SSD) * jax.nn.silu(z.astype(jnp.float32))
    yg = y.reshape(b, L, SSD_GROUPS, W_SSD // SSD_GROUPS)
    yg = yg * lax.rsqrt(jnp.mean(jnp.square(yg), axis=-1, keepdims=True) + RMS_EPS)
    ssd_out = yg.reshape(b, L, W_SSD) * ssd_norm_w.astype(jnp.float32)
    mixed = jnp.concatenate([pool_out, ssd_out], axis=-1).astype(x.dtype)
    o = jnp.einsum('ble,ed->bld', mixed, w_out).astype(jnp.float32)
    r = ALPHA * x.astype(jnp.float32) + gate[:, None] * o
    x_new = (layer_norm(r) * ln_g.astype(jnp.float32) + ln_b.astype(jnp.float32)).astype(x.dtype)
    return x_new, new_pool, new_conv, new_ssm.reshape(b, SSD_HEADS, SSD_HEAD_DIM, SSD_STATE)


def setup_inputs(seed: int = 0) -> dict:
    key = jax.random.key(seed)
    ks = jax.random.split(key, 24)
    f32 = jnp.float32
    nrm = lambda k, s, sc: jax.random.normal(k, s, f32) * sc
    dt0 = jnp.exp(jax.random.uniform(ks[15], (DEPTH, SSD_HEADS), f32) * (math.log(0.1) - math.log(0.001))
                  + math.log(0.001))
    return {
        "x_prompt": nrm(ks[0], (BATCH, SEQ, D_MODEL), 1.0),
        "x_sample": nrm(ks[1], (DEC_BATCH, DEC_SEQ, D_MODEL), 1.0),
        "state_pool": nrm(ks[2], (DEPTH, DEC_BATCH, POOL_HIST, W_POOL), 1.0),
        "state_conv": nrm(ks[3], (DEPTH, DEC_BATCH, CONV_W - 1, CONV_DIM), 1.0),
        "state_ssm": nrm(ks[4], (DEPTH, DEC_BATCH, SSD_HEADS, SSD_HEAD_DIM, SSD_STATE), 0.1),
        "c_prompt": nrm(ks[5], (BATCH, D_MODEL), 1.0),
        "c_sample": nrm(ks[6], (DEC_BATCH, D_MODEL), 1.0),
        "w_ada": nrm(ks[7], (DEPTH, D_MODEL, 3 * D_MODEL), 0.5 * D_MODEL ** -0.5),
        "b_ada": nrm(ks[8], (DEPTH, 3 * D_MODEL), 0.01),
        "w_in": nrm(ks[9], (DEPTH, D_MODEL, IN_DIM), D_MODEL ** -0.5),
        "w_pool": nrm(ks[10], (DEPTH, N_POOL_GROUPS, POOL_GW, POOL_GW), POOL_GW ** -0.5),
        "pool_scale": 1.0 + nrm(ks[11], (DEPTH, W_POOL), 0.02),
        "conv_w": nrm(ks[12], (DEPTH, CONV_W, CONV_DIM), 0.4),
        "conv_b": nrm(ks[13], (DEPTH, CONV_DIM), 0.01),
        "dt_bias": dt0 + jnp.log(-jnp.expm1(-dt0)),
        "a_log": jnp.log(jax.random.uniform(ks[16], (DEPTH, SSD_HEADS), f32, 1.0, 16.0)),
        "d_skip": 1.0 + nrm(ks[17], (DEPTH, SSD_HEADS), 0.1),
        "ssd_norm_w": 1.0 + nrm(ks[18], (DEPTH, W_SSD), 0.02),
        "w_out": nrm(ks[19], (DEPTH, D_INNER, D_MODEL), BETA * D_INNER ** -0.5),
        "ln_g": 1.0 + nrm(ks[20], (DEPTH, D_MODEL), 0.02),
        "ln_b": nrm(ks[21], (DEPTH, D_MODEL), 0.01),
    }


def reference(x_prompt, x_sample, state_pool, state_conv, state_ssm, c_prompt, c_sample,
              w_ada, b_ada, w_in, w_pool, pool_scale, conv_w, conv_b,
              dt_bias, a_log, d_skip, ssd_norm_w, w_out, ln_g, ln_b):
    nb = x_prompt.shape[0]
    yp, ys = x_prompt, x_sample
    pp, pc, psm, sp, sc, ssm = [], [], [], [], [], []
    for l in range(DEPTH):
        params = (w_ada[l], b_ada[l], w_in[l], w_pool[l], pool_scale[l], conv_w[l], conv_b[l],
                  dt_bias[l], a_log[l], d_skip[l], ssd_norm_w[l], w_out[l], ln_g[l], ln_b[l])
        yp, a1, a2, a3 = trunk_layer(
            yp, c_prompt,
            jnp.zeros((nb, POOL_HIST, W_POOL), x_prompt.dtype),
            jnp.zeros((nb, CONV_W - 1, CONV_DIM), x_prompt.dtype),
            jnp.zeros((nb, SSD_HEADS, SSD_HEAD_DIM, SSD_STATE), jnp.float32),
            0, *params)
        ys, b1, b2, b3 = trunk_layer(ys, c_sample, state_pool[l], state_conv[l], state_ssm[l],
                                     PAST_LEN, *params)
        pp.append(a1); pc.append(a2); psm.append(a3)
        sp.append(b1); sc.append(b2); ssm.append(b3)
    return (yp, ys, jnp.stack(pp), jnp.stack(pc), jnp.stack(psm),
            jnp.stack(sp), jnp.stack(sc), jnp.stack(ssm))
```

```python
import functools

import jax
import jax.numpy as jnp
from jax import lax
from jax.experimental import pallas as pl
from jax.experimental.pallas import tpu as pltpu

F32 = jnp.float32
BF16 = jnp.bfloat16

D_MODEL = 4096
DEPTH = 2
D_INNER = 2 * D_MODEL
W_POOL = D_INNER // 4
POOL_WINDOWS = (2, 4, 8, 16)
POOL_GW = W_POOL // len(POOL_WINDOWS)
POOL_HIST = max(POOL_WINDOWS) - 1
W_SSD = D_INNER - W_POOL
HEAD_DIM = 64
HEADS = W_SSD // HEAD_DIM
GROUPS = 8
HPG = HEADS // GROUPS
GROUP_W = HPG * HEAD_DIM
STATE = 128
CONV_W = 4
CONV_DIM = W_SSD + 2 * GROUPS * STATE
MAIN_DIM = 2 * W_POOL + W_SSD + CONV_DIM
ALPHA = (2 * DEPTH) ** 0.25
LN_EPS = 1e-5
RMS_EPS = 1e-5

LANES = 128
POOL_BUF = 16
CONV_BUF = 8
MIX_T = 128
PAIRS_PER_GROUP = GROUP_W // LANES
VMEM_LIMIT = 56 * 1024 * 1024

OFF_U, OFF_G, OFF_Z, OFF_XBC = 0, W_POOL, 2 * W_POOL, 2 * W_POOL + W_SSD


def _silu(x):
    return x / (1.0 + jnp.exp(-x))


def _layer_norm(x):
    mu = jnp.mean(x, axis=-1, keepdims=True)
    xc = x - mu
    var = jnp.mean(xc * xc, axis=-1, keepdims=True)
    return xc * lax.rsqrt(var + LN_EPS)


def _ada_kernel(c_ref, w_ref, b_ref, o_ref):
    sc = _silu(c_ref[...]).astype(BF16)
    o_ref[...] = jnp.dot(sc, w_ref[...].astype(BF16), preferred_element_type=F32) + b_ref[...]


def _ada(c_all, w_ada, b_ada, tn=512):
    rows = c_all.shape[0]
    n = w_ada.shape[-1]
    return pl.pallas_call(
        _ada_kernel,
        grid=(DEPTH, n // tn),
        in_specs=[
            pl.BlockSpec((rows, D_MODEL), lambda l, j: (0, 0)),
            pl.BlockSpec((None, D_MODEL, tn), lambda l, j: (l, 0, j)),
            pl.BlockSpec((None, 1, tn), lambda l, j: (l, 0, j)),
        ],
        out_specs=pl.BlockSpec((None, rows, tn), lambda l, j: (l, 0, j)),
        out_shape=jax.ShapeDtypeStruct((DEPTH, rows, n), F32),
        compiler_params=pltpu.CompilerParams(
            dimension_semantics=("arbitrary", "arbitrary"), vmem_limit_bytes=VMEM_LIMIT),
        name="ada_mod",
    )(c_all, w_ada, b_ada.reshape(DEPTH, 1, n))


def _inproj_kernel(x_ref, sc_ref, sh_ref, w_ref, wdt_ref, o_ref, dt_ref, h_ref, *, sub):
    @pl.when(pl.program_id(1) == 0)
    def _():
        n_mod = sc_ref.shape[0]
        rows_per_mod = x_ref.shape[0] // n_mod
        for r in range(n_mod):
            for s0 in range(0, rows_per_mod, sub):
                rs = slice(r * rows_per_mod + s0, r * rows_per_mod + s0 + min(sub, rows_per_mod))
                h = _layer_norm(x_ref[rs, :]) * (1.0 + sc_ref[r:r + 1, :]) + sh_ref[r:r + 1, :]
                h_ref[rs, :] = h.astype(BF16)
        dt_ref[...] = jnp.dot(h_ref[...], wdt_ref[...], preferred_element_type=F32)

    o_ref[...] = jnp.dot(h_ref[...], w_ref[...], preferred_element_type=F32)


def _inproj(x2d, scale3, shift3, w_in_b, w_dt_b, layer, tm, tn):
    m = x2d.shape[0]
    n_mod = scale3.shape[1]
    tiles_per_mod_block = (m // tm) // scale3.shape[0]
    mod_spec = pl.BlockSpec((None, n_mod, D_MODEL), lambda i, j: (i // tiles_per_mod_block, 0, 0))
    return pl.pallas_call(
        functools.partial(_inproj_kernel, sub=128),
        grid=(m // tm, MAIN_DIM // tn),
        in_specs=[
            pl.BlockSpec((tm, D_MODEL), lambda i, j: (i, 0)),
            mod_spec,
            mod_spec,
            pl.BlockSpec((None, D_MODEL, tn), lambda i, j: (layer, 0, j)),
            pl.BlockSpec((None, D_MODEL, LANES), lambda i, j: (layer, 0, 0)),
        ],
        out_specs=[
            pl.BlockSpec((tm, tn), lambda i, j: (i, j)),
            pl.BlockSpec((tm, LANES), lambda i, j: (i, 0)),
        ],
        out_shape=[
            jax.ShapeDtypeStruct((m, MAIN_DIM), F32),
            jax.ShapeDtypeStruct((m, LANES), F32),
        ],
        scratch_shapes=[pltpu.VMEM((tm, D_MODEL), BF16)],
        compiler_params=pltpu.CompilerParams(
            dimension_semantics=("parallel", "arbitrary"), vmem_limit_bytes=VMEM_LIMIT),
        name="in_proj",
    )(x2d, scale3, shift3, w_in_b, w_dt_b)


def _outproj_kernel(m_ref, w_ref, x_ref, g_ref, lng_ref, lnb_ref, o_ref, *, nk, sub):
    k = pl.program_id(1)
    part = jnp.dot(m_ref[...], w_ref[...], preferred_element_type=F32)

    @pl.when(k == 0)
    def _():
        o_ref[...] = part

    @pl.when(k > 0)
    def _():
        o_ref[...] += part

    @pl.when(k == nk - 1)
    def _():
        n_mod = g_ref.shape[0]
        rows_per_mod = x_ref.shape[0] // n_mod
        for r in range(n_mod):
            for s0 in range(0, rows_per_mod, sub):
                rs = slice(r * rows_per_mod + s0, r * rows_per_mod + s0 + min(sub, rows_per_mod))
                res = ALPHA * x_ref[rs, :] + g_ref[r:r + 1, :] * o_ref[rs, :]
                o_ref[rs, :] = _layer_norm(res) * lng_ref[...] + lnb_ref[...]


def _outproj(mixed, w_out_b, x2d, gate3, ln_g, ln_b, layer, tm, tk):
    m = x2d.shape[0]
    n_mod = gate3.shape[1]
    tiles_per_mod_block = (m // tm) // gate3.shape[0]
    nk = D_INNER // tk
    vec_spec = pl.BlockSpec((None, 1, D_MODEL), lambda i, k: (layer, 0, 0))
    return pl.pallas_call(
        functools.partial(_outproj_kernel, nk=nk, sub=128),
        grid=(m // tm, nk),
        in_specs=[
            pl.BlockSpec((tm, tk), lambda i, k: (i, k)),
            pl.BlockSpec((None, tk, D_MODEL), lambda i, k: (layer, k, 0)),
            pl.BlockSpec((tm, D_MODEL), lambda i, k: (i, 0), pipeline_mode=pl.Buffered(1)),
            pl.BlockSpec((None, n_mod, D_MODEL), lambda i, k: (i // tiles_per_mod_block, 0, 0)),
            vec_spec,
            vec_spec,
        ],
        out_specs=pl.BlockSpec((tm, D_MODEL), lambda i, k: (i, 0)),
        out_shape=jax.ShapeDtypeStruct((m, D_MODEL), F32),
        compiler_params=pltpu.CompilerParams(
            dimension_semantics=("parallel", "arbitrary"), vmem_limit_bytes=VMEM_LIMIT),
        name="out_proj",
    )(mixed, w_out_b, x2d, gate3, ln_g.reshape(DEPTH, 1, D_MODEL), ln_b.reshape(DEPTH, 1, D_MODEL))


def _split3(a):
    hi = a.astype(BF16)
    r1 = a - hi.astype(F32)
    mid = r1.astype(BF16)
    lo = (r1 - mid.astype(F32)).astype(BF16)
    return hi, mid, lo


def _mixer_kernel(*refs, tv, nchunks, start_pos, has_init, layer):
    if has_init:
        (proj_ref, dtr_ref, pool0_ref, conv0_ref, ssm0_ref, *rest) = refs
    else:
        (proj_ref, dtr_ref, *rest) = refs
    (wpool_ref, pscale_ref, convw_ref, convb_ref, dtb_ref, alog_ref, dskip_ref, normw_ref,
     mixed_ref, npool_ref, nconv_ref, nssm_ref, poolbuf, convbuf, cv_ref, h_ref) = rest
    t = MIX_T
    c = pl.program_id(1)

    @pl.when(c == 0)
    def _init():
        if has_init:
            poolbuf[0:POOL_BUF, :] = pool0_ref[...]
            convbuf[0:CONV_BUF, :] = conv0_ref[...]
            for g in range(GROUPS):
                for j in range(PAIRS_PER_GROUP):
                    r0 = (g * HPG + 2 * j) * HEAD_DIM
                    h_ref[g, :, j * LANES:(j + 1) * LANES] = ssm0_ref[r0:r0 + LANES, :].T
        else:
            poolbuf[0:POOL_BUF, :] = jnp.zeros((POOL_BUF, W_POOL), F32)
            convbuf[0:CONV_BUF, :] = jnp.zeros((CONV_BUF, CONV_DIM), F32)
            h_ref[...] = jnp.zeros(h_ref.shape, F32)
        if tv < t:
            cv_ref[tv:t, :] = jnp.zeros((t - tv, CONV_DIM), F32)

    poolbuf[POOL_BUF:POOL_BUF + tv, :] = proj_ref[:, OFF_U:OFF_U + W_POOL]
    row = lax.broadcasted_iota(jnp.int32, (tv, 1), 0)
    pos = start_pos + c * tv + row
    for g, w in enumerate(POOL_WINDOWS):
        cols = slice(g * POOL_GW, (g + 1) * POOL_GW)
        u_g = poolbuf[POOL_BUF:POOL_BUF + tv, cols]
        s = u_g
        for back in range(1, w):
            s = s + poolbuf[POOL_BUF - back:POOL_BUF - back + tv, cols]
        cnt = jnp.minimum(pos + 1, w).astype(F32)
        pooled = (s / cnt - u_g).astype(BF16)
        po = jnp.dot(pooled, wpool_ref[g], preferred_element_type=F32)
        gate = proj_ref[:, OFF_G + g * POOL_GW:OFF_G + (g + 1) * POOL_GW]
        mixed_ref[:, cols] = (po * pscale_ref[:, cols] * _silu(gate)).astype(BF16)
    poolbuf[0:POOL_BUF, :] = poolbuf[tv:tv + POOL_BUF, :]

    slab = 1024
    for c0 in range(0, CONV_DIM, slab):
        cs = slice(c0, c0 + slab)
        convbuf[CONV_BUF:CONV_BUF + tv, cs] = proj_ref[:, OFF_XBC + c0:OFF_XBC + c0 + slab]
        acc = convb_ref[:, cs] + convbuf[CONV_BUF:CONV_BUF + tv, cs] * convw_ref[CONV_W - 1:CONV_W, cs]
        for back in range(1, CONV_W):
            acc = acc + (convbuf[CONV_BUF - back:CONV_BUF - back + tv, cs]
                         * convw_ref[CONV_W - 1 - back:CONV_W - back, cs])
        cv_ref[0:tv, cs] = _silu(acc)
        convbuf[0:CONV_BUF, cs] = convbuf[tv:tv + CONV_BUF, cs]

    x_dt = dtr_ref[...] + dtb_ref[...]
    dt = jnp.maximum(x_dt, 0.0) + jnp.log1p(jnp.exp(-jnp.abs(x_dt)))
    if tv < t:
        dt = jnp.concatenate([dt, jnp.zeros((t - tv, LANES), F32)], axis=0)
    a = dt * (-jnp.exp(alog_ref[...]))
    ri = lax.broadcasted_iota(jnp.int32, (t, t), 0)
    ci = lax.broadcasted_iota(jnp.int32, (t, t), 1)
    tril = ri >= ci
    tri = jnp.where(tril, 1.0, 0.0).astype(BF16)
    a_cs = sum(jnp.dot(tri, piece, preferred_element_type=F32) for piece in _split3(a))
    a_cs_t = a_cs.T
    a_last = a_cs[t - 1:t, :]
    ea = jnp.exp(a_cs)
    dec = jnp.exp(a_last - a_cs)
    ea_last = jnp.exp(a_last)
    lo_half = lax.broadcasted_iota(jnp.int32, (1, LANES), 1) < HEAD_DIM

    def pair_bcast(arr, k0):
        return jnp.where(lo_half, arr[:, k0:k0 + 1], arr[:, k0 + 1:k0 + 2])

    for g in range(GROUPS):
        b_g = cv_ref[:, W_SSD + g * STATE:W_SSD + (g + 1) * STATE]
        c_g = cv_ref[:, W_SSD + GROUPS * STATE + g * STATE:W_SSD + GROUPS * STATE + (g + 1) * STATE]
        b_b = b_g.astype(BF16)
        c_b = c_g.astype(BF16)
        cb = lax.dot_general(c_b, b_b, (((1,), (1,)), ((), ())), preferred_element_type=F32)
        h_g = h_ref[g]
        y_off = jnp.dot(c_b, h_g.astype(BF16), preferred_element_type=F32)
        ys, xdecs, cds = [], [], []
        for j in range(PAIRS_PER_GROUP):
            p = g * PAIRS_PER_GROUP + j
            k0 = 2 * p
            pc = slice(p * LANES, (p + 1) * LANES)
            xs_p = cv_ref[:, pc]
            xdt = xs_p * pair_bcast(dt, k0)
            l0 = jnp.where(tril, jnp.exp(a_cs[:, k0:k0 + 1] - a_cs_t[k0:k0 + 1, :]), 0.0)
            l1 = jnp.where(tril, jnp.exp(a_cs[:, k0 + 1:k0 + 2] - a_cs_t[k0 + 1:k0 + 2, :]), 0.0)
            m_pair = jnp.concatenate([(cb * l0).astype(BF16), (cb * l1).astype(BF16)], axis=1)
            xb = xdt.astype(BF16)
            zero = jnp.zeros_like(xb)
            rhs = jnp.concatenate([jnp.where(lo_half, xb, zero), jnp.where(lo_half, zero, xb)], axis=0)
            y_diag = jnp.dot(m_pair, rhs, preferred_element_type=F32)
            ys.append(y_diag + y_off[:, j * LANES:(j + 1) * LANES] * pair_bcast(ea, k0)
                      + dskip_ref[:, pc] * xs_p)
            xdecs.append((xdt * pair_bcast(dec, k0)).astype(BF16))
            cds.append(pair_bcast(ea_last, k0))
        st = jnp.dot(b_g.T.astype(BF16), jnp.concatenate(xdecs, axis=1), preferred_element_type=F32)
        h_ref[g] = jnp.concatenate(cds, axis=1) * h_g + st
        gc = slice(g * GROUP_W, (g + 1) * GROUP_W)
        y_g = jnp.concatenate(ys, axis=1)[0:tv, :]
        y_g = y_g * _silu(proj_ref[:, OFF_Z + g * GROUP_W:OFF_Z + (g + 1) * GROUP_W])
        ms = jnp.mean(y_g * y_g, axis=-1, keepdims=True)
        out = y_g * lax.rsqrt(ms + RMS_EPS) * normw_ref[:, gc]
        mixed_ref[:, W_POOL + g * GROUP_W:W_POOL + (g + 1) * GROUP_W] = out.astype(BF16)

    @pl.when(c == nchunks - 1)
    def _fin():
        npool_ref[...] = poolbuf[0:POOL_BUF, :]
        nconv_ref[...] = convbuf[0:CONV_BUF, :]
        for g in range(GROUPS):
            for j in range(PAIRS_PER_GROUP):
                r0 = (g * HPG + 2 * j) * HEAD_DIM
                nssm_ref[r0:r0 + LANES, :] = h_ref[g, :, j * LANES:(j + 1) * LANES].T


def _mixer(proj, dt_raw, init, params, nb, seq, start_pos):
    tv = min(seq, MIX_T)
    nchunks = seq // tv
    has_init = init is not None
    row_map = lambda b, c: (b * nchunks + c, 0)
    const2 = lambda b, c: (0, 0)
    per_b = lambda b, c: (b, 0, 0)
    in_specs = [pl.BlockSpec((tv, MAIN_DIM), row_map), pl.BlockSpec((tv, LANES), row_map)]
    args = [proj, dt_raw]
    if has_init:
        in_specs += [pl.BlockSpec((None, POOL_BUF, W_POOL), per_b),
                     pl.BlockSpec((None, CONV_BUF, CONV_DIM), per_b),
                     pl.BlockSpec((None, W_SSD, STATE), per_b)]
        args += list(init)
    w_pool_b, pool_scale, conv_w, conv_b, dt_bias, a_log, d_skip_x, norm_w = params
    in_specs += [
        pl.BlockSpec((len(POOL_WINDOWS), POOL_GW, POOL_GW), lambda b, c: (0, 0, 0)),
        pl.BlockSpec((1, W_POOL), const2),
        pl.BlockSpec((CONV_W, CONV_DIM), const2),
        pl.BlockSpec((1, CONV_DIM), const2),
        pl.BlockSpec((1, LANES), const2),
        pl.BlockSpec((1, LANES), const2),
        pl.BlockSpec((1, W_SSD), const2),
        pl.BlockSpec((1, W_SSD), const2),
    ]
    args += [w_pool_b, pool_scale, conv_w, conv_b, dt_bias, a_log, d_skip_x, norm_w]
    return pl.pallas_call(
        functools.partial(_mixer_kernel, tv=tv, nchunks=nchunks, start_pos=start_pos,
                          has_init=has_init, layer=0),
        grid=(nb, nchunks),
        in_specs=in_specs,
        out_specs=[
            pl.BlockSpec((tv, D_INNER), row_map),
            pl.BlockSpec((None, POOL_BUF, W_POOL), per_b),
            pl.BlockSpec((None, CONV_BUF, CONV_DIM), per_b),
            pl.BlockSpec((None, W_SSD, STATE), per_b),
        ],
        out_shape=[
            jax.ShapeDtypeStruct((nb * seq, D_INNER), BF16),
            jax.ShapeDtypeStruct((nb, POOL_BUF, W_POOL), F32),
            jax.ShapeDtypeStruct((nb, CONV_BUF, CONV_DIM), F32),
            jax.ShapeDtypeStruct((nb, W_SSD, STATE), F32),
        ],
        scratch_shapes=[
            pltpu.VMEM((POOL_BUF + tv, W_POOL), F32),
            pltpu.VMEM((CONV_BUF + tv, CONV_DIM), F32),
            pltpu.VMEM((MIX_T, CONV_DIM), F32),
            pltpu.VMEM((GROUPS, STATE, GROUP_W), F32),
        ],
        compiler_params=pltpu.CompilerParams(
            dimension_semantics=("parallel", "arbitrary"), vmem_limit_bytes=VMEM_LIMIT),
        name="mixer_init" if has_init else "mixer_zero",
    )(*args)


def _pad_rows_front(a, rows):
    pad = rows - a.shape[-2]
    return jnp.pad(a, [(0, 0)] * (a.ndim - 2) + [(pad, 0), (0, 0)])


def _pad_lanes(a):
    return jnp.pad(a, [(0, 0)] * (a.ndim - 1) + [(0, LANES - a.shape[-1])])


def kernel(x_prompt, x_sample, state_pool, state_conv, state_ssm, c_prompt, c_sample,
           w_ada, b_ada, w_in, w_pool, pool_scale, conv_w, conv_b,
           dt_bias, a_log, d_skip, ssd_norm_w, w_out, ln_g, ln_b):
    nbp, seq_p, _ = x_prompt.shape
    nbs, seq_s, _ = x_sample.shape
    past_len = 1024

    n_c = nbp + nbs
    c_rows = -(-n_c // 8) * 8
    c_all = jnp.concatenate([c_prompt, c_sample, jnp.zeros((c_rows - n_c, D_MODEL), F32)], axis=0)
    mod = _ada(c_all, w_ada, b_ada)

    w_in_b = w_in.astype(BF16)
    w_dt_b = _pad_lanes(w_in[:, :, MAIN_DIM:]).astype(BF16)
    w_out_b = w_out.astype(BF16)
    w_pool_b = w_pool.astype(BF16)
    d_skip_x = jnp.repeat(d_skip, HEAD_DIM, axis=-1)

    pool0 = _pad_rows_front(state_pool, POOL_BUF)
    conv0 = _pad_rows_front(state_conv, CONV_BUF)
    ssm0 = state_ssm.reshape(DEPTH, nbs, W_SSD, STATE)

    xp = x_prompt.reshape(nbp * seq_p, D_MODEL)
    xs = x_sample.reshape(nbs * seq_s, D_MODEL)
    outs = {k: [] for k in ("pp", "pc", "psm", "sp", "sc", "ssm")}
    for l in range(DEPTH):
        shift, scale, gate = (mod[l, :, i * D_MODEL:(i + 1) * D_MODEL] for i in range(3))
        params = (w_pool_b[l], pool_scale[l][None], conv_w[l], conv_b[l][None],
                  _pad_lanes(dt_bias[l][None]), _pad_lanes(a_log[l][None]),
                  d_skip_x[l][None], ssd_norm_w[l][None])

        sc3, sh3, g3 = (v[0:nbp].reshape(nbp, 1, D_MODEL) for v in (scale, shift, gate))
        proj, dt_raw = _inproj(xp, sc3, sh3, w_in_b, w_dt_b, l, tm=512, tn=1024)
        mixed, npool, nconv, nssm = _mixer(proj, dt_raw, None, params, nbp, seq_p, 0)
        xp = _outproj(mixed, w_out_b, xp, g3, ln_g, ln_b, l, tm=512, tk=512)
        outs["pp"].append(npool[:, 1:])
        outs["pc"].append(nconv[:, CONV_BUF - (CONV_W - 1):])
        outs["psm"].append(nssm.reshape(nbp, HEADS, HEAD_DIM, STATE))

        sc3, sh3, g3 = (v[nbp:n_c].reshape(1, nbs, D_MODEL) for v in (scale, shift, gate))
        proj, dt_raw = _inproj(xs, sc3, sh3, w_in_b, w_dt_b, l, tm=nbs * seq_s, tn=1024)
        mixed, npool, nconv, nssm = _mixer(proj, dt_raw, (pool0[l], conv0[l], ssm0[l]), params,
                                           nbs, seq_s, past_len)
        xs = _outproj(mixed, w_out_b, xs, g3, ln_g, ln_b, l, tm=nbs * seq_s, tk=512)
        outs["sp"].append(npool[:, 1:])
        outs["sc"].append(nconv[:, CONV_BUF - (CONV_W - 1):])
        outs["ssm"].append(nssm.reshape(nbs, HEADS, HEAD_DIM, STATE))

    return (xp.reshape(nbp, seq_p, D_MODEL), xs.reshape(nbs, seq_s, D_MODEL),
            jnp.stack(outs["pp"]), jnp.stack(outs["pc"]), jnp.stack(outs["psm"]),
            jnp.stack(outs["sp"]), jnp.stack(outs["sc"]), jnp.stack(outs["ssm"]))
```

```python
import functools
import math

import jax
import jax.numpy as jnp
from jax import lax
from jax.experimental import pallas as pl
from jax.experimental.pallas import tpu as pltpu

F32 = jnp.float32
BF16 = jnp.bfloat16

D_MODEL = 4096
DEPTH = 2
D_INNER = 2 * D_MODEL
W_POOL = D_INNER // 4
POOL_WINDOWS = (2, 4, 8, 16)
POOL_GW = W_POOL // len(POOL_WINDOWS)
POOL_HIST = max(POOL_WINDOWS) - 1
W_SSD = D_INNER - W_POOL
HEAD_DIM = 64
HEADS = W_SSD // HEAD_DIM
GROUPS = 8
HPG = HEADS // GROUPS
GROUP_W = HPG * HEAD_DIM
STATE = 128
CONV_W = 4
CONV_DIM = W_SSD + 2 * GROUPS * STATE
MAIN_DIM = 2 * W_POOL + W_SSD + CONV_DIM
ALPHA = (2 * DEPTH) ** 0.25
LN_EPS = 1e-5
RMS_EPS = 1e-5
LOG2E = math.log2(math.e)

LANES = 128
POOL_BUF = 16
CONV_BUF = 8
MIX_T = 128
PAIRS_PER_GROUP = GROUP_W // LANES
VMEM_LIMIT = 56 * 1024 * 1024

OFF_U, OFF_G, OFF_Z, OFF_XBC = 0, W_POOL, 2 * W_POOL, 2 * W_POOL + W_SSD


def _silu(x):
    return x * (0.5 + 0.5 * jnp.tanh(0.5 * x))


def _layer_norm(x):
    mu = jnp.mean(x, axis=-1, keepdims=True)
    xc = x - mu
    var = jnp.mean(xc * xc, axis=-1, keepdims=True)
    return xc * lax.rsqrt(var + LN_EPS)


def _ada_kernel(c_ref, w_ref, b_ref, o_ref):
    sc = _silu(c_ref[...]).astype(BF16)
    o_ref[...] = jnp.dot(sc, w_ref[...].astype(BF16), preferred_element_type=F32) + b_ref[...]


def _ada(c_all, w_ada, b_ada, tn=512):
    rows = c_all.shape[0]
    n = w_ada.shape[-1]
    return pl.pallas_call(
        _ada_kernel,
        grid=(DEPTH, n // tn),
        in_specs=[
            pl.BlockSpec((rows, D_MODEL), lambda l, j: (0, 0)),
            pl.BlockSpec((None, D_MODEL, tn), lambda l, j: (l, 0, j)),
            pl.BlockSpec((None, 1, tn), lambda l, j: (l, 0, j)),
        ],
        out_specs=pl.BlockSpec((None, rows, tn), lambda l, j: (l, 0, j)),
        out_shape=jax.ShapeDtypeStruct((DEPTH, rows, n), F32),
        compiler_params=pltpu.CompilerParams(
            dimension_semantics=("arbitrary", "arbitrary"), vmem_limit_bytes=VMEM_LIMIT),
        name="ada_mod",
    )(c_all, w_ada, b_ada.reshape(DEPTH, 1, n))


def _lnmod_kernel(x_ref, sc_ref, sh_ref, wdt_ref, h_ref, dt_ref, *, sub):
    n_mod = sc_ref.shape[0]
    rows_per_mod = x_ref.shape[0] // n_mod
    step = min(sub, rows_per_mod)
    for r in range(n_mod):
        for s0 in range(0, rows_per_mod, step):
            rs = slice(r * rows_per_mod + s0, r * rows_per_mod + s0 + step)
            h = _layer_norm(x_ref[rs, :]) * (1.0 + sc_ref[r:r + 1, :]) + sh_ref[r:r + 1, :]
            h_ref[rs, :] = h.astype(BF16)
    dt_ref[...] = jnp.dot(h_ref[...], wdt_ref[...], preferred_element_type=F32)


def _lnmod(x2d, scale3, shift3, w_dt_b, layer, tm):
    m = x2d.shape[0]
    n_mod = scale3.shape[1]
    tiles_per_mod_block = (m // tm) // scale3.shape[0]
    mod_spec = pl.BlockSpec((None, n_mod, D_MODEL), lambda i: (i // tiles_per_mod_block, 0, 0))
    return pl.pallas_call(
        functools.partial(_lnmod_kernel, sub=128),
        grid=(m // tm,),
        in_specs=[
            pl.BlockSpec((tm, D_MODEL), lambda i: (i, 0)),
            mod_spec,
            mod_spec,
            pl.BlockSpec((None, D_MODEL, LANES), lambda i: (layer, 0, 0)),
        ],
        out_specs=[
            pl.BlockSpec((tm, D_MODEL), lambda i: (i, 0)),
            pl.BlockSpec((tm, LANES), lambda i: (i, 0)),
        ],
        out_shape=[
            jax.ShapeDtypeStruct((m, D_MODEL), BF16),
            jax.ShapeDtypeStruct((m, LANES), F32),
        ],
        compiler_params=pltpu.CompilerParams(
            dimension_semantics=("parallel",), vmem_limit_bytes=VMEM_LIMIT),
        name="ln_mod",
    )(x2d, scale3, shift3, w_dt_b)


def _inproj_kernel(h_ref, w_ref, o_ref, wb_ref):
    @pl.when(pl.program_id(1) == 0)
    def _():
        wb_ref[...] = w_ref[...].astype(BF16)

    o_ref[...] = jnp.dot(h_ref[...], wb_ref[...], preferred_element_type=F32)


def _inproj(h2d, w_in, layer, tm, tn):
    m = h2d.shape[0]
    return pl.pallas_call(
        _inproj_kernel,
        grid=(MAIN_DIM // tn, m // tm),
        in_specs=[
            pl.BlockSpec((tm, D_MODEL), lambda j, i: (i, 0)),
            pl.BlockSpec((None, D_MODEL, tn), lambda j, i: (layer, 0, j)),
        ],
        out_specs=pl.BlockSpec((tm, tn), lambda j, i: (i, j)),
        out_shape=jax.ShapeDtypeStruct((m, MAIN_DIM), F32),
        scratch_shapes=[pltpu.VMEM((D_MODEL, tn), BF16)],
        compiler_params=pltpu.CompilerParams(
            dimension_semantics=("arbitrary", "arbitrary"), vmem_limit_bytes=VMEM_LIMIT),
        name="in_proj",
    )(h2d, w_in)


def _outproj_kernel(m_ref, w_ref, x_ref, g_ref, lng_ref, lnb_ref, o_ref, acc_ref, *, nj, tn, sub):
    j = pl.program_id(1)
    n_mod = g_ref.shape[0]
    tm = m_ref.shape[0]
    rows_per_mod = tm // n_mod
    o = jnp.dot(m_ref[...], w_ref[...], preferred_element_type=F32)
    for r in range(n_mod):
        rs = slice(r * rows_per_mod, (r + 1) * rows_per_mod)
        acc_ref[j, rs, :] = ALPHA * x_ref[rs, :] + g_ref[r:r + 1, :] * o[rs, :]

    @pl.when(j == nj - 1)
    def _():
        inv_d = 1.0 / (nj * tn)
        for s0 in range(0, tm, sub):
            rs = slice(s0, s0 + sub)
            total = acc_ref[0, rs, :].sum(axis=-1, keepdims=True)
            for jj in range(1, nj):
                total = total + acc_ref[jj, rs, :].sum(axis=-1, keepdims=True)
            mu = total * inv_d
            sq = jnp.zeros_like(mu)
            for jj in range(nj):
                d = acc_ref[jj, rs, :] - mu
                sq = sq + (d * d).sum(axis=-1, keepdims=True)
            inv = lax.rsqrt(sq * inv_d + LN_EPS)
            for jj in range(nj):
                cs = slice(jj * tn, (jj + 1) * tn)
                o_ref[rs, cs] = (acc_ref[jj, rs, :] - mu) * inv * lng_ref[:, cs] + lnb_ref[:, cs]


def _outproj(mixed, w_out_b, x2d, gate3, ln_g, ln_b, layer, tm, tn):
    m = x2d.shape[0]
    n_mod = gate3.shape[1]
    tiles_per_mod_block = (m // tm) // gate3.shape[0]
    nj = D_MODEL // tn
    vec_spec = pl.BlockSpec((None, 1, D_MODEL), lambda i, j: (layer, 0, 0))
    return pl.pallas_call(
        functools.partial(_outproj_kernel, nj=nj, tn=tn, sub=min(128, tm)),
        grid=(m // tm, nj),
        in_specs=[
            pl.BlockSpec((tm, D_INNER), lambda i, j: (i, 0)),
            pl.BlockSpec((None, D_INNER, tn), lambda i, j: (layer, 0, j)),
            pl.BlockSpec((tm, tn), lambda i, j: (i, j)),
            pl.BlockSpec((None, n_mod, tn), lambda i, j: (i // tiles_per_mod_block, 0, j)),
            vec_spec,
            vec_spec,
        ],
        out_specs=pl.BlockSpec((tm, D_MODEL), lambda i, j: (i, 0)),
        out_shape=jax.ShapeDtypeStruct((m, D_MODEL), F32),
        scratch_shapes=[pltpu.VMEM((nj, tm, tn), F32)],
        compiler_params=pltpu.CompilerParams(
            dimension_semantics=("parallel", "arbitrary"), vmem_limit_bytes=VMEM_LIMIT),
        name="out_proj",
    )(mixed, w_out_b, x2d, gate3, ln_g.reshape(DEPTH, 1, D_MODEL), ln_b.reshape(DEPTH, 1, D_MODEL))


def _split3(a):
    hi = a.astype(BF16)
    r1 = a - hi.astype(F32)
    mid = r1.astype(BF16)
    lo = (r1 - mid.astype(F32)).astype(BF16)
    return hi, mid, lo


def _mixer_kernel(*refs, tv, nchunks, start_pos, has_init):
    if has_init:
        (proj_ref, dtr_ref, pool0_ref, conv0_ref, ssm0_ref, *rest) = refs
    else:
        (proj_ref, dtr_ref, *rest) = refs
    (wpool_ref, pscale_ref, convw_ref, convb_ref, dtb_ref, alog_ref, dskip_ref, normw_ref,
     mixed_ref, npool_ref, nconv_ref, nssm_ref, poolbuf, convbuf, cv_ref, h_ref) = rest
    t = MIX_T
    c = pl.program_id(1)

    @pl.when(c == 0)
    def _init():
        if has_init:
            poolbuf[0:POOL_BUF, :] = pool0_ref[...]
            convbuf[0:CONV_BUF, :] = conv0_ref[...]
            for g in range(GROUPS):
                for j in range(PAIRS_PER_GROUP):
                    r0 = (g * HPG + 2 * j) * HEAD_DIM
                    h_ref[g, :, j * LANES:(j + 1) * LANES] = ssm0_ref[r0:r0 + LANES, :].T
        else:
            poolbuf[0:POOL_BUF, :] = jnp.zeros((POOL_BUF, W_POOL), F32)
            convbuf[0:CONV_BUF, :] = jnp.zeros((CONV_BUF, CONV_DIM), F32)
            h_ref[...] = jnp.zeros(h_ref.shape, F32)
        if tv < t:
            cv_ref[tv:t, :] = jnp.zeros((t - tv, CONV_DIM), F32)

    poolbuf[POOL_BUF:POOL_BUF + tv, :] = proj_ref[:, OFF_U:OFF_U + W_POOL]
    row = lax.broadcasted_iota(jnp.int32, (tv, 1), 0)
    pos = start_pos + c * tv + row
    for g, w in enumerate(POOL_WINDOWS):
        cols = slice(g * POOL_GW, (g + 1) * POOL_GW)
        u_g = poolbuf[POOL_BUF:POOL_BUF + tv, cols]
        s = u_g
        for back in range(1, w):
            s = s + poolbuf[POOL_BUF - back:POOL_BUF - back + tv, cols]
        cnt = jnp.minimum(pos + 1, w).astype(F32)
        pooled = (s / cnt - u_g).astype(BF16)
        po = jnp.dot(pooled, wpool_ref[g], preferred_element_type=F32)
        gate = proj_ref[:, OFF_G + g * POOL_GW:OFF_G + (g + 1) * POOL_GW]
        mixed_ref[:, cols] = (po * pscale_ref[:, cols] * _silu(gate)).astype(BF16)
    poolbuf[0:POOL_BUF, :] = poolbuf[tv:tv + POOL_BUF, :]

    slab = 1024
    for c0 in range(0, CONV_DIM, slab):
        cs = slice(c0, c0 + slab)
        convbuf[CONV_BUF:CONV_BUF + tv, cs] = proj_ref[:, OFF_XBC + c0:OFF_XBC + c0 + slab]
        acc = convb_ref[:, cs] + convbuf[CONV_BUF:CONV_BUF + tv, cs] * convw_ref[CONV_W - 1:CONV_W, cs]
        for back in range(1, CONV_W):
            acc = acc + (convbuf[CONV_BUF - back:CONV_BUF - back + tv, cs]
                         * convw_ref[CONV_W - 1 - back:CONV_W - back, cs])
        cv_ref[0:tv, cs] = _silu(acc)
        convbuf[0:CONV_BUF, cs] = convbuf[tv:tv + CONV_BUF, cs]

    x_dt = dtr_ref[...] + dtb_ref[...]
    dt = jnp.maximum(x_dt, 0.0) + jnp.log1p(jnp.exp(-jnp.abs(x_dt)))
    if tv < t:
        dt = jnp.concatenate([dt, jnp.zeros((t - tv, LANES), F32)], axis=0)
    a = dt * (-LOG2E * jnp.exp(alog_ref[...]))
    ri = lax.broadcasted_iota(jnp.int32, (t, t), 0)
    ci = lax.broadcasted_iota(jnp.int32, (t, t), 1)
    tril = ri >= ci
    tri = jnp.where(tril, 1.0, 0.0).astype(BF16)
    a_cs = sum(jnp.dot(tri, piece, preferred_element_type=F32) for piece in _split3(a))
    a_cs_t = a_cs.T
    a_last = a_cs[t - 1:t, :]
    ea = jnp.exp2(a_cs)
    dec = jnp.exp2(a_last - a_cs)
    ea_last = jnp.exp2(a_last)
    lo_half = lax.broadcasted_iota(jnp.int32, (1, LANES), 1) < HEAD_DIM

    def pair_bcast(arr, k0):
        return jnp.where(lo_half, arr[:, k0:k0 + 1], arr[:, k0 + 1:k0 + 2])

    for g in range(GROUPS):
        b_g = cv_ref[:, W_SSD + g * STATE:W_SSD + (g + 1) * STATE]
        c_g = cv_ref[:, W_SSD + GROUPS * STATE + g * STATE:W_SSD + GROUPS * STATE + (g + 1) * STATE]
        b_b = b_g.astype(BF16)
        c_b = c_g.astype(BF16)
        cb = lax.dot_general(c_b, b_b, (((1,), (1,)), ((), ())), preferred_element_type=F32)
        h_g = h_ref[g]
        y_off = jnp.dot(c_b, h_g.astype(BF16), preferred_element_type=F32)
        ys, xdecs, cds = [], [], []
        for j in range(PAIRS_PER_GROUP):
            p = g * PAIRS_PER_GROUP + j
            k0 = 2 * p
            pc = slice(p * LANES, (p + 1) * LANES)
            xs_p = cv_ref[:, pc]
            xdt = xs_p * pair_bcast(dt, k0)
            l0 = jnp.where(tril, jnp.exp2(a_cs[:, k0:k0 + 1] - a_cs_t[k0:k0 + 1, :]), 0.0)
            l1 = jnp.where(tril, jnp.exp2(a_cs[:, k0 + 1:k0 + 2] - a_cs_t[k0 + 1:k0 + 2, :]), 0.0)
            m_pair = jnp.concatenate([(cb * l0).astype(BF16), (cb * l1).astype(BF16)], axis=1)
            xb = xdt.astype(BF16)
            zero = jnp.zeros_like(xb)
            rhs = jnp.concatenate([jnp.where(lo_half, xb, zero), jnp.where(lo_half, zero, xb)], axis=0)
            y_diag = jnp.dot(m_pair, rhs, preferred_element_type=F32)
            ys.append(y_diag + y_off[:, j * LANES:(j + 1) * LANES] * pair_bcast(ea, k0)
                      + dskip_ref[:, pc] * xs_p)
            xdecs.append((xdt * pair_bcast(dec, k0)).astype(BF16))
            cds.append(pair_bcast(ea_last, k0))
        st = jnp.dot(b_g.T.astype(BF16), jnp.concatenate(xdecs, axis=1), preferred_element_type=F32)
        h_ref[g] = jnp.concatenate(cds, axis=1) * h_g + st
        gc = slice(g * GROUP_W, (g + 1) * GROUP_W)
        y_g = jnp.concatenate(ys, axis=1)[0:tv, :]
        y_g = y_g * _silu(proj_ref[:, OFF_Z + g * GROUP_W:OFF_Z + (g + 1) * GROUP_W])
        ms = jnp.mean(y_g * y_g, axis=-1, keepdims=True)
        out = y_g * lax.rsqrt(ms + RMS_EPS) * normw_ref[:, gc]
        mixed_ref[:, W_POOL + g * GROUP_W:W_POOL + (g + 1) * GROUP_W] = out.astype(BF16)

    @pl.when(c == nchunks - 1)
    def _fin():
        npool_ref[...] = poolbuf[0:POOL_BUF, :]
        nconv_ref[...] = convbuf[0:CONV_BUF, :]
        for g in range(GROUPS):
            for j in range(PAIRS_PER_GROUP):
                r0 = (g * HPG + 2 * j) * HEAD_DIM
                nssm_ref[r0:r0 + LANES, :] = h_ref[g, :, j * LANES:(j + 1) * LANES].T


def _mixer(proj, dt_raw, init, params, layer, nb, seq, start_pos):
    tv = min(seq, MIX_T)
    nchunks = seq // tv
    has_init = init is not None
    row_map = lambda b, c: (b * nchunks + c, 0)
    per_layer2 = lambda b, c: (layer, 0, 0)
    per_b = lambda b, c: (b, 0, 0)
    per_layer_b = lambda b, c: (layer, b, 0, 0)
    in_specs = [pl.BlockSpec((tv, MAIN_DIM), row_map), pl.BlockSpec((tv, LANES), row_map)]
    args = [proj, dt_raw]
    if has_init:
        in_specs += [pl.BlockSpec((None, None, POOL_BUF, W_POOL), per_layer_b),
                     pl.BlockSpec((None, None, CONV_BUF, CONV_DIM), per_layer_b),
                     pl.BlockSpec((None, None, W_SSD, STATE), per_layer_b)]
        args += list(init)
    in_specs += [
        pl.BlockSpec((None, len(POOL_WINDOWS), POOL_GW, POOL_GW), lambda b, c: (layer, 0, 0, 0)),
        pl.BlockSpec((None, 1, W_POOL), per_layer2),
        pl.BlockSpec((None, CONV_W, CONV_DIM), per_layer2),
        pl.BlockSpec((None, 1, CONV_DIM), per_layer2),
        pl.BlockSpec((None, 1, LANES), per_layer2),
        pl.BlockSpec((None, 1, LANES), per_layer2),
        pl.BlockSpec((None, 1, W_SSD), per_layer2),
        pl.BlockSpec((None, 1, W_SSD), per_layer2),
    ]
    args += list(params)
    return pl.pallas_call(
        functools.partial(_mixer_kernel, tv=tv, nchunks=nchunks, start_pos=start_pos, has_init=has_init),
        grid=(nb, nchunks),
        in_specs=in_specs,
        out_specs=[
            pl.BlockSpec((tv, D_INNER), row_map),
            pl.BlockSpec((None, POOL_BUF, W_POOL), per_b),
            pl.BlockSpec((None, CONV_BUF, CONV_DIM), per_b),
            pl.BlockSpec((None, W_SSD, STATE), per_b),
        ],
        out_shape=[
            jax.ShapeDtypeStruct((nb * seq, D_INNER), BF16),
            jax.ShapeDtypeStruct((nb, POOL_BUF, W_POOL), F32),
            jax.ShapeDtypeStruct((nb, CONV_BUF, CONV_DIM), F32),
            jax.ShapeDtypeStruct((nb, W_SSD, STATE), F32),
        ],
        scratch_shapes=[
            pltpu.VMEM((POOL_BUF + tv, W_POOL), F32),
            pltpu.VMEM((CONV_BUF + tv, CONV_DIM), F32),
            pltpu.VMEM((MIX_T, CONV_DIM), F32),
            pltpu.VMEM((GROUPS, STATE, GROUP_W), F32),
        ],
        compiler_params=pltpu.CompilerParams(
            dimension_semantics=("parallel", "arbitrary"), vmem_limit_bytes=VMEM_LIMIT),
        name="mixer_init" if has_init else "mixer_zero",
    )(*args)


def _pad_rows_front(a, rows):
    pad = rows - a.shape[-2]
    return jnp.pad(a, [(0, 0)] * (a.ndim - 2) + [(pad, 0), (0, 0)])


def _pad_lanes(a):
    return jnp.pad(a, [(0, 0)] * (a.ndim - 1) + [(0, LANES - a.shape[-1])])


def kernel(x_prompt, x_sample, state_pool, state_conv, state_ssm, c_prompt, c_sample,
           w_ada, b_ada, w_in, w_pool, pool_scale, conv_w, conv_b,
           dt_bias, a_log, d_skip, ssd_norm_w, w_out, ln_g, ln_b):
    nbp, seq_p, _ = x_prompt.shape
    nbs, seq_s, _ = x_sample.shape
    past_len = 1024

    n_c = nbp + nbs
    c_rows = -(-n_c // 8) * 8
    c_all = jnp.concatenate([c_prompt, c_sample, jnp.zeros((c_rows - n_c, D_MODEL), F32)], axis=0)
    mod = _ada(c_all, w_ada, b_ada)

    w_dt_b = _pad_lanes(w_in[:, :, MAIN_DIM:]).astype(BF16)
    w_out_b = w_out.astype(BF16)
    params = (w_pool.astype(BF16), pool_scale[:, None], conv_w, conv_b[:, None],
              _pad_lanes(dt_bias)[:, None], _pad_lanes(a_log)[:, None],
              jnp.repeat(d_skip, HEAD_DIM, axis=-1)[:, None], ssd_norm_w[:, None])
    init = (_pad_rows_front(state_pool, POOL_BUF),
            _pad_rows_front(state_conv, CONV_BUF),
            state_ssm.reshape(DEPTH, nbs, W_SSD, STATE))

    xp = x_prompt.reshape(nbp * seq_p, D_MODEL)
    xs = x_sample.reshape(nbs * seq_s, D_MODEL)
    outs = {k: [] for k in ("pp", "pc", "psm", "sp", "sc", "ssm")}
    for l in range(DEPTH):
        shift, scale, gate = (mod[l, :, i * D_MODEL:(i + 1) * D_MODEL] for i in range(3))

        sc3, sh3, g3 = (v[0:nbp].reshape(nbp, 1, D_MODEL) for v in (scale, shift, gate))
        h, dt_raw = _lnmod(xp, sc3, sh3, w_dt_b, l, tm=512)
        proj = _inproj(h, w_in, l, tm=1024, tn=512)
        mixed, npool, nconv, nssm = _mixer(proj, dt_raw, None, params, l, nbp, seq_p, 0)
        xp = _outproj(mixed, w_out_b, xp, g3, ln_g, ln_b, l, tm=512, tn=256)
        outs["pp"].append(npool[:, 1:])
        outs["pc"].append(nconv[:, CONV_BUF - (CONV_W - 1):])
        outs["psm"].append(nssm.reshape(nbp, HEADS, HEAD_DIM, STATE))

        sc3, sh3, g3 = (v[nbp:n_c].reshape(1, nbs, D_MODEL) for v in (scale, shift, gate))
        h, dt_raw = _lnmod(xs, sc3, sh3, w_dt_b, l, tm=nbs * seq_s)
        proj = _inproj(h, w_in, l, tm=nbs * seq_s, tn=512)
        mixed, npool, nconv, nssm = _mixer(proj, dt_raw, init, params, l, nbs, seq_s, past_len)
        xs = _outproj(mixed, w_out_b, xs, g3, ln_g, ln_b, l, tm=nbs * seq_s, tn=256)
        outs["sp"].append(npool[:, 1:])
        outs["sc"].append(nconv[:, CONV_BUF - (CONV_W - 1):])
        outs["ssm"].append(nssm.reshape(nbs, HEADS, HEAD_DIM, STATE))

    return (xp.reshape(nbp, seq_p, D_MODEL), xs.reshape(nbs, seq_s, D_MODEL),
            jnp.stack(outs["pp"]), jnp.stack(outs["pc"]), jnp.stack(outs["psm"]),
            jnp.stack(outs["sp"]), jnp.stack(outs["sc"]), jnp.stack(outs["ssm"]))
```

```python
import functools
import math

import jax
import jax.numpy as jnp
from jax import lax
from jax.experimental import pallas as pl
from jax.experimental.pallas import tpu as pltpu

F32 = jnp.float32
BF16 = jnp.bfloat16

D_MODEL = 4096
DEPTH = 2
D_INNER = 2 * D_MODEL
W_POOL = D_INNER // 4
POOL_WINDOWS = (2, 4, 8, 16)
POOL_GW = W_POOL // len(POOL_WINDOWS)
POOL_HIST = max(POOL_WINDOWS) - 1
W_SSD = D_INNER - W_POOL
HEAD_DIM = 64
HEADS = W_SSD // HEAD_DIM
GROUPS = 8
HPG = HEADS // GROUPS
GROUP_W = HPG * HEAD_DIM
STATE = 128
CONV_W = 4
CONV_DIM = W_SSD + 2 * GROUPS * STATE
MAIN_DIM = 2 * W_POOL + W_SSD + CONV_DIM
ALPHA = (2 * DEPTH) ** 0.25
LN_EPS = 1e-5
RMS_EPS = 1e-5
LOG2E = math.log2(math.e)

LANES = 128
POOL_BUF = 16
CONV_BUF = 8
MIX_T = 128
PAIRS_PER_GROUP = GROUP_W // LANES
assert MIX_T == STATE == LANES and 2 * HEAD_DIM == LANES
VMEM_LIMIT = 56 * 1024 * 1024

OFF_U, OFF_G, OFF_Z, OFF_XBC = 0, W_POOL, 2 * W_POOL, 2 * W_POOL + W_SSD


def _silu(x):
    return x * (0.5 + 0.5 * jnp.tanh(0.5 * x))


def _layer_norm(x):
    mu = jnp.mean(x, axis=-1, keepdims=True)
    xc = x - mu
    var = jnp.mean(xc * xc, axis=-1, keepdims=True)
    return xc * lax.rsqrt(var + LN_EPS)


def _ada_kernel(c_ref, w_ref, b_ref, o_ref):
    sc = _silu(c_ref[...]).astype(BF16)
    o_ref[...] = jnp.dot(sc, w_ref[...].astype(BF16), preferred_element_type=F32) + b_ref[...]


def _ada(c_all, w_ada, b_ada, tn=512):
    rows = c_all.shape[0]
    n = w_ada.shape[-1]
    return pl.pallas_call(
        _ada_kernel,
        grid=(DEPTH, n // tn),
        in_specs=[
            pl.BlockSpec((rows, D_MODEL), lambda l, j: (0, 0)),
            pl.BlockSpec((None, D_MODEL, tn), lambda l, j: (l, 0, j)),
            pl.BlockSpec((None, 1, tn), lambda l, j: (l, 0, j)),
        ],
        out_specs=pl.BlockSpec((None, rows, tn), lambda l, j: (l, 0, j)),
        out_shape=jax.ShapeDtypeStruct((DEPTH, rows, n), F32),
        compiler_params=pltpu.CompilerParams(
            dimension_semantics=("arbitrary", "arbitrary"), vmem_limit_bytes=VMEM_LIMIT),
        name="ada_mod",
    )(c_all, w_ada, b_ada.reshape(DEPTH, 1, n))


def _lnmod_kernel(x_ref, sc_ref, sh_ref, wdt_ref, h_ref, dt_ref, *, sub):
    n_mod = sc_ref.shape[0]
    rows_per_mod = x_ref.shape[0] // n_mod
    step = min(sub, rows_per_mod)
    for r in range(n_mod):
        for s0 in range(0, rows_per_mod, step):
            rs = slice(r * rows_per_mod + s0, r * rows_per_mod + s0 + step)
            h = _layer_norm(x_ref[rs, :]) * (1.0 + sc_ref[r:r + 1, :]) + sh_ref[r:r + 1, :]
            h_ref[rs, :] = h.astype(BF16)
    dt_ref[...] = jnp.dot(h_ref[...], wdt_ref[...], preferred_element_type=F32)


def _lnmod(x2d, scale3, shift3, w_dt_b, layer, tm):
    m = x2d.shape[0]
    n_mod = scale3.shape[1]
    tiles_per_mod_block = (m // tm) // scale3.shape[0]
    mod_spec = pl.BlockSpec((None, n_mod, D_MODEL), lambda i: (i // tiles_per_mod_block, 0, 0))
    return pl.pallas_call(
        functools.partial(_lnmod_kernel, sub=128),
        grid=(m // tm,),
        in_specs=[
            pl.BlockSpec((tm, D_MODEL), lambda i: (i, 0)),
            mod_spec,
            mod_spec,
            pl.BlockSpec((None, D_MODEL, LANES), lambda i: (layer, 0, 0)),
        ],
        out_specs=[
            pl.BlockSpec((tm, D_MODEL), lambda i: (i, 0)),
            pl.BlockSpec((tm, LANES), lambda i: (i, 0)),
        ],
        out_shape=[
            jax.ShapeDtypeStruct((m, D_MODEL), BF16),
            jax.ShapeDtypeStruct((m, LANES), F32),
        ],
        compiler_params=pltpu.CompilerParams(
            dimension_semantics=("parallel",), vmem_limit_bytes=VMEM_LIMIT),
        name="ln_mod",
    )(x2d, scale3, shift3, w_dt_b)


def _inproj_kernel(*refs, mode, n_seg, has_init, tiles_per_seq, sub):
    if mode == "conv":
        if has_init:
            h_ref, w_ref, cw_ref, cb_ref, hist_ref, o_ref, st_ref, wb_ref, buf = refs
        else:
            h_ref, w_ref, cw_ref, cb_ref, o_ref, st_ref, wb_ref, buf = refs
    else:
        h_ref, w_ref, o_ref, wb_ref = refs
    i = pl.program_id(1)

    @pl.when(i == 0)
    def _():
        wb_ref[...] = w_ref[...].astype(BF16)

    res = lax.dot_general(h_ref[...], wb_ref[...], (((1,), (1,)), ((), ())),
                          preferred_element_type=F32)
    if mode == "raw":
        o_ref[...] = res
    elif mode == "silu":
        o_ref[...] = _silu(res)
    else:
        seg = h_ref.shape[0] // n_seg
        if not has_init:
            @pl.when(i % tiles_per_seq == 0)
            def _():
                buf[0, 0:CONV_BUF, :] = jnp.zeros((CONV_BUF, buf.shape[-1]), F32)
        for s in range(n_seg):
            if has_init:
                buf[s, 0:CONV_BUF, :] = hist_ref[s]
            buf[s, CONV_BUF:CONV_BUF + seg, :] = res[s * seg:(s + 1) * seg, :]
            for r0 in range(0, seg, sub):
                n = min(sub, seg)
                acc = cb_ref[...] + buf[s, CONV_BUF + r0:CONV_BUF + r0 + n, :] * cw_ref[CONV_W - 1:CONV_W, :]
                for back in range(1, CONV_W):
                    acc = acc + (buf[s, CONV_BUF + r0 - back:CONV_BUF + r0 - back + n, :]
                                 * cw_ref[CONV_W - 1 - back:CONV_W - back, :])
                o_ref[s * seg + r0:s * seg + r0 + n, :] = _silu(acc)
            st_ref[s] = buf[s, seg:seg + CONV_BUF, :]
            if not has_init:
                buf[s, 0:CONV_BUF, :] = buf[s, seg:seg + CONV_BUF, :]


def _inproj(h2d, w_in_t, layer, col0, width, mode, tm, tn, conv=None, n_seg=1, tiles_per_seq=1):
    m = h2d.shape[0]
    j0 = col0 // tn
    in_specs = [
        pl.BlockSpec((tm, D_MODEL), lambda j, i: (i, 0)),
        pl.BlockSpec((None, tn, D_MODEL), lambda j, i: (layer, j0 + j, 0)),
    ]
    args = [h2d, w_in_t]
    out_specs = pl.BlockSpec((tm, tn), lambda j, i: (i, j))
    out_shape = jax.ShapeDtypeStruct((m, width), F32)
    scratch = [pltpu.VMEM((tn, D_MODEL), BF16)]
    has_init = False
    if mode == "conv":
        conv_w, conv_b, hist = conv
        has_init = hist is not None
        in_specs += [pl.BlockSpec((None, CONV_W, tn), lambda j, i: (layer, 0, j)),
                     pl.BlockSpec((None, 1, tn), lambda j, i: (layer, 0, j))]
        args += [conv_w, conv_b]
        if has_init:
            in_specs.append(pl.BlockSpec((None, n_seg, CONV_BUF, tn), lambda j, i: (layer, 0, 0, j)))
            args.append(hist)
        n_seq = (m // tm) * n_seg // tiles_per_seq
        out_specs = [out_specs,
                     pl.BlockSpec((n_seg, CONV_BUF, tn), lambda j, i: (i // tiles_per_seq, 0, j))]
        out_shape = [out_shape, jax.ShapeDtypeStruct((n_seq, CONV_BUF, width), F32)]
        scratch.append(pltpu.VMEM((n_seg, CONV_BUF + tm // n_seg, tn), F32))
    return pl.pallas_call(
        functools.partial(_inproj_kernel, mode=mode, n_seg=n_seg, has_init=has_init,
                          tiles_per_seq=tiles_per_seq, sub=256),
        grid=(width // tn, m // tm),
        in_specs=in_specs,
        out_specs=out_specs,
        out_shape=out_shape,
        scratch_shapes=scratch,
        compiler_params=pltpu.CompilerParams(
            dimension_semantics=("arbitrary", "arbitrary"), vmem_limit_bytes=VMEM_LIMIT),
        name="in_proj_" + mode,
    )(*args)


def _outproj_kernel(m_ref, w_ref, x_ref, g_ref, lng_ref, lnb_ref, o_ref, acc_ref, *, nj, tn, sub):
    j = pl.program_id(1)
    n_mod = g_ref.shape[0]
    tm = m_ref.shape[0]
    rows_per_mod = tm // n_mod
    o = jnp.dot(m_ref[...], w_ref[...], preferred_element_type=F32)
    for r in range(n_mod):
        rs = slice(r * rows_per_mod, (r + 1) * rows_per_mod)
        acc_ref[j, rs, :] = ALPHA * x_ref[rs, :] + g_ref[r:r + 1, :] * o[rs, :]

    @pl.when(j == nj - 1)
    def _():
        inv_d = 1.0 / (nj * tn)
        for s0 in range(0, tm, sub):
            rs = slice(s0, s0 + sub)
            total = acc_ref[0, rs, :].sum(axis=-1, keepdims=True)
            for jj in range(1, nj):
                total = total + acc_ref[jj, rs, :].sum(axis=-1, keepdims=True)
            mu = total * inv_d
            sq = jnp.zeros_like(mu)
            for jj in range(nj):
                d = acc_ref[jj, rs, :] - mu
                sq = sq + (d * d).sum(axis=-1, keepdims=True)
            inv = lax.rsqrt(sq * inv_d + LN_EPS)
            for jj in range(nj):
                cs = slice(jj * tn, (jj + 1) * tn)
                o_ref[rs, cs] = (acc_ref[jj, rs, :] - mu) * inv * lng_ref[:, cs] + lnb_ref[:, cs]


def _outproj(mixed, w_out_b, x2d, gate3, ln_g, ln_b, layer, tm, tn):
    m = x2d.shape[0]
    n_mod = gate3.shape[1]
    tiles_per_mod_block = (m // tm) // gate3.shape[0]
    nj = D_MODEL // tn
    vec_spec = pl.BlockSpec((None, 1, D_MODEL), lambda i, j: (layer, 0, 0))
    return pl.pallas_call(
        functools.partial(_outproj_kernel, nj=nj, tn=tn, sub=min(128, tm)),
        grid=(m // tm, nj),
        in_specs=[
            pl.BlockSpec((tm, D_INNER), lambda i, j: (i, 0)),
            pl.BlockSpec((None, D_INNER, tn), lambda i, j: (layer, 0, j)),
            pl.BlockSpec((tm, tn), lambda i, j: (i, j)),
            pl.BlockSpec((None, n_mod, tn), lambda i, j: (i // tiles_per_mod_block, 0, j)),
            vec_spec,
            vec_spec,
        ],
        out_specs=pl.BlockSpec((tm, D_MODEL), lambda i, j: (i, 0)),
        out_shape=jax.ShapeDtypeStruct((m, D_MODEL), F32),
        scratch_shapes=[pltpu.VMEM((nj, tm, tn), F32)],
        compiler_params=pltpu.CompilerParams(
            dimension_semantics=("parallel", "arbitrary"), vmem_limit_bytes=VMEM_LIMIT),
        name="out_proj",
    )(mixed, w_out_b, x2d, gate3, ln_g.reshape(DEPTH, 1, D_MODEL), ln_b.reshape(DEPTH, 1, D_MODEL))


def _split3(a):
    hi = a.astype(BF16)
    r1 = a - hi.astype(F32)
    mid = r1.astype(BF16)
    lo = (r1 - mid.astype(F32)).astype(BF16)
    return hi, mid, lo


def _mixer_kernel(*refs, tv, nchunks, start_pos, has_init):
    if has_init:
        (u_ref, gz_ref, xc_ref, dtr_ref, pool0_ref, ssm0_ref, *rest) = refs
    else:
        (u_ref, gz_ref, xc_ref, dtr_ref, *rest) = refs
    (wpool_ref, pscale_ref, dtb_ref, alog_ref, dskip_ref, normw_ref,
     mixed_ref, npool_ref, nssm_ref, poolbuf, h_ref) = rest
    t = MIX_T
    c = pl.program_id(1)

    def pad_rows(v):
        if tv == t:
            return v
        return jnp.concatenate([v, jnp.zeros((t - tv, v.shape[1]), v.dtype)], axis=0)

    @pl.when(c == 0)
    def _init():
        if has_init:
            poolbuf[0:POOL_BUF, :] = pool0_ref[...]
            for g in range(GROUPS):
                for j in range(PAIRS_PER_GROUP):
                    r0 = (g * HPG + 2 * j) * HEAD_DIM
                    h_ref[g, :, j * LANES:(j + 1) * LANES] = ssm0_ref[r0:r0 + LANES, :].T
        else:
            poolbuf[0:POOL_BUF, :] = jnp.zeros((POOL_BUF, W_POOL), F32)
            h_ref[...] = jnp.zeros(h_ref.shape, F32)

    poolbuf[POOL_BUF:POOL_BUF + tv, :] = u_ref[...]
    row = lax.broadcasted_iota(jnp.int32, (tv, 1), 0)
    pos = start_pos + c * tv + row
    for g, w in enumerate(POOL_WINDOWS):
        cols = slice(g * POOL_GW, (g + 1) * POOL_GW)
        u_g = poolbuf[POOL_BUF:POOL_BUF + tv, cols]
        s = u_g
        for back in range(1, w):
            s = s + poolbuf[POOL_BUF - back:POOL_BUF - back + tv, cols]
        cnt = jnp.minimum(pos + 1, w).astype(F32)
        pooled = (s / cnt - u_g).astype(BF16)
        po = jnp.dot(pooled, wpool_ref[g], preferred_element_type=F32)
        mixed_ref[:, cols] = (po * pscale_ref[:, cols] * gz_ref[:, cols]).astype(BF16)
    poolbuf[0:POOL_BUF, :] = poolbuf[tv:tv + POOL_BUF, :]

    x_dt = dtr_ref[...] + dtb_ref[...]
    dt = jnp.maximum(x_dt, 0.0) + jnp.log1p(jnp.exp(-jnp.abs(x_dt)))
    if tv < t:
        dt = jnp.concatenate([dt, jnp.zeros((t - tv, LANES), F32)], axis=0)
    a = dt * (-LOG2E * jnp.exp(alog_ref[...]))
    ri = lax.broadcasted_iota(jnp.int32, (t, t), 0)
    ci = lax.broadcasted_iota(jnp.int32, (t, t), 1)
    tril = ri >= ci
    tri = jnp.where(tril, 1.0, 0.0).astype(BF16)
    a_cs = sum(jnp.dot(tri, piece, preferred_element_type=F32) for piece in _split3(a))
    a_last = a_cs[t - 1:t, :]
    ea_last = jnp.exp2(a_last)
    a_cs_t = a_cs.T
    dt_t = dt.T
    w_t = (jnp.exp2(a_last - a_cs) * dt).T
    lo_half = lax.broadcasted_iota(jnp.int32, (1, LANES), 1) < HEAD_DIM

    def block_diag(v):
        zero = jnp.zeros_like(v)
        return jnp.concatenate([jnp.where(lo_half, v, zero), jnp.where(lo_half, zero, v)], axis=0)

    for g in range(GROUPS):
        b_g = pad_rows(xc_ref[:, W_SSD + g * STATE:W_SSD + (g + 1) * STATE])
        c_g = pad_rows(xc_ref[:, W_SSD + GROUPS * STATE + g * STATE:W_SSD + GROUPS * STATE + (g + 1) * STATE])
        cb = lax.dot_general(c_g.astype(BF16), b_g.astype(BF16), (((1,), (1,)), ((), ())),
                             preferred_element_type=F32)
        b_t = b_g.T
        ys = []
        for j in range(PAIRS_PER_GROUP):
            p = g * PAIRS_PER_GROUP + j
            pc = slice(p * LANES, (p + 1) * LANES)
            jc = slice(j * LANES, (j + 1) * LANES)
            xs_p = pad_rows(xc_ref[:, pc])
            h_p = h_ref[g, :, jc]
            bd_x = block_diag(xs_p.astype(BF16))
            bd_h = block_diag(h_p.astype(BF16))
            m_parts, e_parts, b_parts = [], [], []
            for k in (2 * p, 2 * p + 1):
                col = jnp.broadcast_to(a_cs[:, k:k + 1], (t, t))
                lmat = jnp.where(tril, jnp.exp2(col - a_cs_t[k:k + 1, :]), 0.0)
                m_parts.append((cb * lmat * dt_t[k:k + 1, :]).astype(BF16))
                e_parts.append((c_g * jnp.exp2(col)).astype(BF16))
                b_parts.append((b_t * w_t[k:k + 1, :]).astype(BF16))
            y_mm = jnp.dot(jnp.concatenate(m_parts + e_parts, axis=1),
                           jnp.concatenate([bd_x, bd_h], axis=0), preferred_element_type=F32)
            ys.append(y_mm + dskip_ref[:, pc] * xs_p)
            st = jnp.dot(jnp.concatenate(b_parts, axis=1), bd_x, preferred_element_type=F32)
            cd = jnp.where(lo_half, ea_last[:, 2 * p:2 * p + 1], ea_last[:, 2 * p + 1:2 * p + 2])
            h_ref[g, :, jc] = cd * h_p + st
        gc = slice(g * GROUP_W, (g + 1) * GROUP_W)
        y_g = jnp.concatenate(ys, axis=1)[0:tv, :]
        y_g = y_g * gz_ref[:, W_POOL + g * GROUP_W:W_POOL + (g + 1) * GROUP_W]
        ms = jnp.mean(y_g * y_g, axis=-1, keepdims=True)
        out = y_g * lax.rsqrt(ms + RMS_EPS) * normw_ref[:, gc]
        mixed_ref[:, W_POOL + g * GROUP_W:W_POOL + (g + 1) * GROUP_W] = out.astype(BF16)

    @pl.when(c == nchunks - 1)
    def _fin():
        npool_ref[...] = poolbuf[0:POOL_BUF, :]
        for g in range(GROUPS):
            for j in range(PAIRS_PER_GROUP):
                r0 = (g * HPG + 2 * j) * HEAD_DIM
                nssm_ref[r0:r0 + LANES, :] = h_ref[g, :, j * LANES:(j + 1) * LANES].T


def _mixer(u, gz, xc, dt_raw, init, params, layer, nb, seq, start_pos):
    tv = min(seq, MIX_T)
    nchunks = seq // tv
    has_init = init is not None
    row_map = lambda b, c: (b * nchunks + c, 0)
    per_layer2 = lambda b, c: (layer, 0, 0)
    per_b = lambda b, c: (b, 0, 0)
    per_layer_b = lambda b, c: (layer, b, 0, 0)
    in_specs = [pl.BlockSpec((tv, W_POOL), row_map), pl.BlockSpec((tv, W_POOL + W_SSD), row_map),
                pl.BlockSpec((tv, CONV_DIM), row_map), pl.BlockSpec((tv, LANES), row_map)]
    args = [u, gz, xc, dt_raw]
    if has_init:
        in_specs += [pl.BlockSpec((None, None, POOL_BUF, W_POOL), per_layer_b),
                     pl.BlockSpec((None, None, W_SSD, STATE), per_layer_b)]
        args += list(init)
    in_specs += [
        pl.BlockSpec((None, len(POOL_WINDOWS), POOL_GW, POOL_GW), lambda b, c: (layer, 0, 0, 0)),
        pl.BlockSpec((None, 1, W_POOL), per_layer2),
        pl.BlockSpec((None, 1, LANES), per_layer2),
        pl.BlockSpec((None, 1, LANES), per_layer2),
        pl.BlockSpec((None, 1, W_SSD), per_layer2),
        pl.BlockSpec((None, 1, W_SSD), per_layer2),
    ]
    args += list(params)
    return pl.pallas_call(
        functools.partial(_mixer_kernel, tv=tv, nchunks=nchunks, start_pos=start_pos, has_init=has_init),
        grid=(nb, nchunks),
        in_specs=in_specs,
        out_specs=[
            pl.BlockSpec((tv, D_INNER), row_map),
            pl.BlockSpec((None, POOL_BUF, W_POOL), per_b),
            pl.BlockSpec((None, W_SSD, STATE), per_b),
        ],
        out_shape=[
            jax.ShapeDtypeStruct((nb * seq, D_INNER), BF16),
            jax.ShapeDtypeStruct((nb, POOL_BUF, W_POOL), F32),
            jax.ShapeDtypeStruct((nb, W_SSD, STATE), F32),
        ],
        scratch_shapes=[
            pltpu.VMEM((POOL_BUF + tv, W_POOL), F32),
            pltpu.VMEM((GROUPS, STATE, GROUP_W), F32),
        ],
        compiler_params=pltpu.CompilerParams(
            dimension_semantics=("parallel", "arbitrary"), vmem_limit_bytes=VMEM_LIMIT),
        name="mixer_init" if has_init else "mixer_zero",
    )(*args)


def _pad_rows_front(a, rows):
    pad = rows - a.shape[-2]
    return jnp.pad(a, [(0, 0)] * (a.ndim - 2) + [(pad, 0), (0, 0)])


def _pad_lanes(a):
    return jnp.pad(a, [(0, 0)] * (a.ndim - 1) + [(0, LANES - a.shape[-1])])


def kernel(x_prompt, x_sample, state_pool, state_conv, state_ssm, c_prompt, c_sample,
           w_ada, b_ada, w_in, w_pool, pool_scale, conv_w, conv_b,
           dt_bias, a_log, d_skip, ssd_norm_w, w_out, ln_g, ln_b):
    nbp, seq_p, _ = x_prompt.shape
    nbs, seq_s, _ = x_sample.shape
    past_len = 1024

    n_c = nbp + nbs
    c_rows = -(-n_c // 8) * 8
    c_all = jnp.concatenate([c_prompt, c_sample, jnp.zeros((c_rows - n_c, D_MODEL), F32)], axis=0)
    mod = _ada(c_all, w_ada, b_ada)

    w_in_t = jnp.swapaxes(w_in, 1, 2)
    w_dt_b = _pad_lanes(w_in[:, :, MAIN_DIM:]).astype(BF16)
    w_out_b = w_out.astype(BF16)
    conv_b3 = conv_b[:, None]
    params = (w_pool.astype(BF16), pool_scale[:, None],
              _pad_lanes(dt_bias)[:, None], _pad_lanes(a_log)[:, None],
              jnp.repeat(d_skip, HEAD_DIM, axis=-1)[:, None], ssd_norm_w[:, None])
    conv0 = _pad_rows_front(state_conv, CONV_BUF)
    init = (_pad_rows_front(state_pool, POOL_BUF),
            state_ssm.reshape(DEPTH, nbs, W_SSD, STATE))

    def in_projections(h, l, tm, hist, n_seg, tiles_per_seq):
        u = _inproj(h, w_in_t, l, OFF_U, W_POOL, "raw", tm, 512)
        gz = _inproj(h, w_in_t, l, OFF_G, W_POOL + W_SSD, "silu", tm, 512)
        xc, nconv = _inproj(h, w_in_t, l, OFF_XBC, CONV_DIM, "conv", tm, 512,
                            conv=(conv_w, conv_b3, hist), n_seg=n_seg, tiles_per_seq=tiles_per_seq)
        return u, gz, xc, nconv

    xp = x_prompt.reshape(nbp * seq_p, D_MODEL)
    xs = x_sample.reshape(nbs * seq_s, D_MODEL)
    outs = {k: [] for k in ("pp", "pc", "psm", "sp", "sc", "ssm")}
    tm_p = 1024
    for l in range(DEPTH):
        shift, scale, gate = (mod[l, :, i * D_MODEL:(i + 1) * D_MODEL] for i in range(3))

        sc3, sh3, g3 = (v[0:nbp].reshape(nbp, 1, D_MODEL) for v in (scale, shift, gate))
        h, dt_raw = _lnmod(xp, sc3, sh3, w_dt_b, l, tm=512)
        u, gz, xc, nconv = in_projections(h, l, tm_p, None, 1, seq_p // tm_p)
        mixed, npool, nssm = _mixer(u, gz, xc, dt_raw, None, params, l, nbp, seq_p, 0)
        xp = _outproj(mixed, w_out_b, xp, g3, ln_g, ln_b, l, tm=512, tn=256)
        outs["pp"].append(npool[:, 1:])
        outs["pc"].append(nconv[:, CONV_BUF - (CONV_W - 1):])
        outs["psm"].append(nssm.reshape(nbp, HEADS, HEAD_DIM, STATE))

        sc3, sh3, g3 = (v[nbp:n_c].reshape(1, nbs, D_MODEL) for v in (scale, shift, gate))
        h, dt_raw = _lnmod(xs, sc3, sh3, w_dt_b, l, tm=nbs * seq_s)
        u, gz, xc, nconv = in_projections(h, l, nbs * seq_s, conv0, nbs, 1)
        mixed, npool, nssm = _mixer(u, gz, xc, dt_raw, init, params, l, nbs, seq_s, past_len)
        xs = _outproj(mixed, w_out_b, xs, g3, ln_g, ln_b, l, tm=nbs * seq_s, tn=256)
        outs["sp"].append(npool[:, 1:])
        outs["sc"].append(nconv[:, CONV_BUF - (CONV_W - 1):])
        outs["ssm"].append(nssm.reshape(nbs, HEADS, HEAD_DIM, STATE))

    return (xp.reshape(nbp, seq_p, D_MODEL), xs.reshape(nbs, seq_s, D_MODEL),
            jnp.stack(outs["pp"]), jnp.stack(outs["pc"]), jnp.stack(outs["psm"]),
            jnp.stack(outs["sp"]), jnp.stack(outs["sc"]), jnp.stack(outs["ssm"]))
```

```python
import functools
import math

import jax
import jax.numpy as jnp
from jax import lax
from jax.experimental import pallas as pl
from jax.experimental.pallas import tpu as pltpu

F32 = jnp.float32
BF16 = jnp.bfloat16

D_MODEL = 4096
DEPTH = 2
D_INNER = 2 * D_MODEL
W_POOL = D_INNER // 4
POOL_WINDOWS = (2, 4, 8, 16)
POOL_GW = W_POOL // len(POOL_WINDOWS)
POOL_HIST = max(POOL_WINDOWS) - 1
W_SSD = D_INNER - W_POOL
HEAD_DIM = 64
HEADS = W_SSD // HEAD_DIM
GROUPS = 8
HPG = HEADS // GROUPS
GROUP_W = HPG * HEAD_DIM
STATE = 128
CONV_W = 4
CONV_DIM = W_SSD + 2 * GROUPS * STATE
MAIN_DIM = 2 * W_POOL + W_SSD + CONV_DIM
ALPHA = (2 * DEPTH) ** 0.25
LN_EPS = 1e-5
RMS_EPS = 1e-5
LOG2E = math.log2(math.e)

LANES = 128
POOL_BUF = 16
CONV_BUF = 8
MIX_T = 128
PAIRS_PER_GROUP = GROUP_W // LANES
assert MIX_T == STATE == LANES and 2 * HEAD_DIM == LANES
VMEM_LIMIT = 56 * 1024 * 1024

OFF_U, OFF_G, OFF_Z, OFF_XBC = 0, W_POOL, 2 * W_POOL, 2 * W_POOL + W_SSD


def _silu_of_double(h):
    return h * jnp.tanh(h) + h


def _silu(x):
    return _silu_of_double(0.5 * x)


def _layer_norm(x):
    mu = jnp.mean(x, axis=-1, keepdims=True)
    xc = x - mu
    var = jnp.mean(xc * xc, axis=-1, keepdims=True)
    return xc * lax.rsqrt(var + LN_EPS)


def _ada_kernel(c_ref, w_ref, b_ref, o_ref):
    sc = _silu(c_ref[...]).astype(BF16)
    o_ref[...] = jnp.dot(sc, w_ref[...].astype(BF16), preferred_element_type=F32) + b_ref[...]


def _ada(c_all, w_ada, b_ada, tn=512):
    rows = c_all.shape[0]
    n = w_ada.shape[-1]
    return pl.pallas_call(
        _ada_kernel,
        grid=(DEPTH, n // tn),
        in_specs=[
            pl.BlockSpec((rows, D_MODEL), lambda l, j: (0, 0)),
            pl.BlockSpec((None, D_MODEL, tn), lambda l, j: (l, 0, j)),
            pl.BlockSpec((None, 1, tn), lambda l, j: (l, 0, j)),
        ],
        out_specs=pl.BlockSpec((None, rows, tn), lambda l, j: (l, 0, j)),
        out_shape=jax.ShapeDtypeStruct((DEPTH, rows, n), F32),
        compiler_params=pltpu.CompilerParams(
            dimension_semantics=("arbitrary", "arbitrary"), vmem_limit_bytes=VMEM_LIMIT),
        name="ada_mod",
    )(c_all, w_ada, b_ada.reshape(DEPTH, 1, n))


def _lnmod_kernel(x_ref, sc_ref, sh_ref, wdt_ref, h_ref, dt_ref, *, sub):
    n_mod = sc_ref.shape[0]
    rows_per_mod = x_ref.shape[0] // n_mod
    step = min(sub, rows_per_mod)
    for r in range(n_mod):
        for s0 in range(0, rows_per_mod, step):
            rs = slice(r * rows_per_mod + s0, r * rows_per_mod + s0 + step)
            h = _layer_norm(x_ref[rs, :]) * (1.0 + sc_ref[r:r + 1, :]) + sh_ref[r:r + 1, :]
            h_ref[rs, :] = h.astype(BF16)
    dt_ref[...] = jnp.dot(h_ref[...], wdt_ref[...], preferred_element_type=F32)


def _lnmod(x2d, scale3, shift3, w_dt_b, layer, tm):
    m = x2d.shape[0]
    n_mod = scale3.shape[1]
    tiles_per_mod_block = (m // tm) // scale3.shape[0]
    mod_spec = pl.BlockSpec((None, n_mod, D_MODEL), lambda i: (i // tiles_per_mod_block, 0, 0))
    return pl.pallas_call(
        functools.partial(_lnmod_kernel, sub=128),
        grid=(m // tm,),
        in_specs=[
            pl.BlockSpec((tm, D_MODEL), lambda i: (i, 0)),
            mod_spec,
            mod_spec,
            pl.BlockSpec((None, D_MODEL, LANES), lambda i: (layer, 0, 0)),
        ],
        out_specs=[
            pl.BlockSpec((tm, D_MODEL), lambda i: (i, 0)),
            pl.BlockSpec((tm, LANES), lambda i: (i, 0)),
        ],
        out_shape=[
            jax.ShapeDtypeStruct((m, D_MODEL), BF16),
            jax.ShapeDtypeStruct((m, LANES), F32),
        ],
        compiler_params=pltpu.CompilerParams(
            dimension_semantics=("parallel",), vmem_limit_bytes=VMEM_LIMIT),
        name="ln_mod",
    )(x2d, scale3, shift3, w_dt_b)


def _inproj_kernel(*refs, mode, n_seg, has_init, tiles_per_seq, sub):
    if mode == "conv":
        if has_init:
            h_ref, w_ref, cw_ref, cb_ref, hist_ref, o_ref, st_ref, wb_ref, buf = refs
        else:
            h_ref, w_ref, cw_ref, cb_ref, o_ref, st_ref, wb_ref, buf = refs
    else:
        h_ref, w_ref, o_ref, wb_ref = refs
    i = pl.program_id(1)

    @pl.when(i == 0)
    def _():
        wb_ref[...] = w_ref[...].astype(BF16)

    def project(r0, n):
        return lax.dot_general(h_ref[r0:r0 + n, :], wb_ref[...], (((1,), (1,)), ((), ())),
                               preferred_element_type=F32)

    tm = h_ref.shape[0]
    if mode == "raw":
        o_ref[...] = project(0, tm)
    elif mode == "silu":
        o_ref[...] = _silu(project(0, tm))
    else:
        seg = tm // n_seg
        half = tm // 2 if (n_seg == 1 and tm >= 2 * sub) else tm
        if not has_init:
            @pl.when(i % tiles_per_seq == 0)
            def _():
                buf[0, 0:CONV_BUF, :] = jnp.zeros((CONV_BUF, buf.shape[-1]), F32)
        for s in range(n_seg):
            if has_init:
                buf[s, 0:CONV_BUF, :] = hist_ref[s]
        cw_half = 0.5 * cw_ref[...]
        cb_half = 0.5 * cb_ref[...]
        for h0 in range(0, tm, half):
            res = project(h0, half)
            for s in range(h0 // seg, max(h0 // seg + 1, (h0 + half) // seg)):
                lo = max(h0, s * seg) - s * seg
                hi = min(h0 + half, (s + 1) * seg) - s * seg
                buf[s, CONV_BUF + lo:CONV_BUF + hi, :] = res[s * seg + lo - h0:s * seg + hi - h0, :]
                for r0 in range(lo, hi, sub):
                    n = min(sub, hi - r0)
                    acc = cb_half + buf[s, CONV_BUF + r0:CONV_BUF + r0 + n, :] * cw_half[CONV_W - 1:CONV_W, :]
                    for back in range(1, CONV_W):
                        acc = acc + (buf[s, CONV_BUF + r0 - back:CONV_BUF + r0 - back + n, :]
                                     * cw_half[CONV_W - 1 - back:CONV_W - back, :])
                    o_ref[s * seg + r0:s * seg + r0 + n, :] = _silu_of_double(acc)
        for s in range(n_seg):
            st_ref[s] = buf[s, seg:seg + CONV_BUF, :]
            if not has_init:
                buf[s, 0:CONV_BUF, :] = buf[s, seg:seg + CONV_BUF, :]


def _inproj(h2d, w_in_t, layer, col0, width, mode, tm, tn, conv=None, n_seg=1, tiles_per_seq=1):
    m = h2d.shape[0]
    j0 = col0 // tn
    in_specs = [
        pl.BlockSpec((tm, D_MODEL), lambda j, i: (i, 0)),
        pl.BlockSpec((None, tn, D_MODEL), lambda j, i: (layer, j0 + j, 0)),
    ]
    args = [h2d, w_in_t]
    out_specs = pl.BlockSpec((tm, tn), lambda j, i: (i, j))
    out_shape = jax.ShapeDtypeStruct((m, width), F32)
    scratch = [pltpu.VMEM((tn, D_MODEL), BF16)]
    has_init = False
    if mode == "conv":
        conv_w, conv_b, hist = conv
        has_init = hist is not None
        in_specs += [pl.BlockSpec((None, CONV_W, tn), lambda j, i: (layer, 0, j)),
                     pl.BlockSpec((None, 1, tn), lambda j, i: (layer, 0, j))]
        args += [conv_w, conv_b]
        if has_init:
            in_specs.append(pl.BlockSpec((None, n_seg, CONV_BUF, tn), lambda j, i: (layer, 0, 0, j)))
            args.append(hist)
        n_seq = (m // tm) * n_seg // tiles_per_seq
        out_specs = [out_specs,
                     pl.BlockSpec((n_seg, CONV_BUF, tn), lambda j, i: (i // tiles_per_seq, 0, j))]
        out_shape = [out_shape, jax.ShapeDtypeStruct((n_seq, CONV_BUF, width), F32)]
        scratch.append(pltpu.VMEM((n_seg, CONV_BUF + tm // n_seg, tn), F32))
    return pl.pallas_call(
        functools.partial(_inproj_kernel, mode=mode, n_seg=n_seg, has_init=has_init,
                          tiles_per_seq=tiles_per_seq, sub=256),
        grid=(width // tn, m // tm),
        in_specs=in_specs,
        out_specs=out_specs,
        out_shape=out_shape,
        scratch_shapes=scratch,
        compiler_params=pltpu.CompilerParams(
            dimension_semantics=("arbitrary", "arbitrary"), vmem_limit_bytes=VMEM_LIMIT),
        name="in_proj_" + mode,
    )(*args)


def _outproj_kernel(m_ref, w_ref, x_ref, g_ref, lng_ref, lnb_ref, o_ref, acc_ref, *, nj, tn, sub):
    j = pl.program_id(1)
    n_mod = g_ref.shape[0]
    tm = m_ref.shape[0]
    rows_per_mod = tm // n_mod
    o = jnp.dot(m_ref[...], w_ref[...], preferred_element_type=F32)
    for r in range(n_mod):
        rs = slice(r * rows_per_mod, (r + 1) * rows_per_mod)
        acc_ref[j, rs, :] = ALPHA * x_ref[rs, :] + g_ref[r:r + 1, :] * o[rs, :]

    @pl.when(j == nj - 1)
    def _():
        inv_d = 1.0 / (nj * tn)
        for s0 in range(0, tm, sub):
            rs = slice(s0, s0 + sub)
            total = acc_ref[0, rs, :].sum(axis=-1, keepdims=True)
            for jj in range(1, nj):
                total = total + acc_ref[jj, rs, :].sum(axis=-1, keepdims=True)
            mu = total * inv_d
            sq = jnp.zeros_like(mu)
            for jj in range(nj):
                d = acc_ref[jj, rs, :] - mu
                sq = sq + (d * d).sum(axis=-1, keepdims=True)
            inv = lax.rsqrt(sq * inv_d + LN_EPS)
            for jj in range(nj):
                cs = slice(jj * tn, (jj + 1) * tn)
                o_ref[rs, cs] = (acc_ref[jj, rs, :] - mu) * inv * lng_ref[:, cs] + lnb_ref[:, cs]


def _outproj(mixed, w_out_b, x2d, gate3, ln_g, ln_b, layer, tm, tn):
    m = x2d.shape[0]
    n_mod = gate3.shape[1]
    tiles_per_mod_block = (m // tm) // gate3.shape[0]
    nj = D_MODEL // tn
    vec_spec = pl.BlockSpec((None, 1, D_MODEL), lambda i, j: (layer, 0, 0))
    return pl.pallas_call(
        functools.partial(_outproj_kernel, nj=nj, tn=tn, sub=min(128, tm)),
        grid=(m // tm, nj),
        in_specs=[
            pl.BlockSpec((tm, D_INNER), lambda i, j: (i, 0)),
            pl.BlockSpec((None, D_INNER, tn), lambda i, j: (layer, 0, j)),
            pl.BlockSpec((tm, tn), lambda i, j: (i, j)),
            pl.BlockSpec((None, n_mod, tn), lambda i, j: (i // tiles_per_mod_block, 0, j)),
            vec_spec,
            vec_spec,
        ],
        out_specs=pl.BlockSpec((tm, D_MODEL), lambda i, j: (i, 0)),
        out_shape=jax.ShapeDtypeStruct((m, D_MODEL), F32),
        scratch_shapes=[pltpu.VMEM((nj, tm, tn), F32)],
        compiler_params=pltpu.CompilerParams(
            dimension_semantics=("parallel", "arbitrary"), vmem_limit_bytes=VMEM_LIMIT),
        name="out_proj",
    )(mixed, w_out_b, x2d, gate3, ln_g.reshape(DEPTH, 1, D_MODEL), ln_b.reshape(DEPTH, 1, D_MODEL))


def _split3(a):
    hi = a.astype(BF16)
    r1 = a - hi.astype(F32)
    mid = r1.astype(BF16)
    lo = (r1 - mid.astype(F32)).astype(BF16)
    return hi, mid, lo


def _mixer_kernel(*refs, tv, nchunks, start_pos, has_init):
    if has_init:
        (u_ref, gz_ref, xc_ref, dtr_ref, pool0_ref, ssm0_ref, *rest) = refs
    else:
        (u_ref, gz_ref, xc_ref, dtr_ref, *rest) = refs
    (wpool_ref, pscale_ref, dtb_ref, alog_ref, dskip_ref, normw_ref,
     mixed_ref, npool_ref, nssm_ref, poolbuf, h_ref) = rest
    t = MIX_T
    c = pl.program_id(1)

    def pad_rows(v):
        if tv == t:
            return v
        return jnp.concatenate([v, jnp.zeros((t - tv, v.shape[1]), v.dtype)], axis=0)

    @pl.when(c == 0)
    def _init():
        if has_init:
            poolbuf[0:POOL_BUF, :] = pool0_ref[...]
            for g in range(GROUPS):
                for j in range(PAIRS_PER_GROUP):
                    r0 = (g * HPG + 2 * j) * HEAD_DIM
                    h_ref[g, :, j * LANES:(j + 1) * LANES] = ssm0_ref[r0:r0 + LANES, :].T
        else:
            poolbuf[0:POOL_BUF, :] = jnp.zeros((POOL_BUF, W_POOL), F32)
            h_ref[...] = jnp.zeros(h_ref.shape, F32)

    poolbuf[POOL_BUF:POOL_BUF + tv, :] = u_ref[...]
    row = lax.broadcasted_iota(jnp.int32, (tv, 1), 0)
    pos = start_pos + c * tv + row
    for g, w in enumerate(POOL_WINDOWS):
        cols = slice(g * POOL_GW, (g + 1) * POOL_GW)
        u_g = poolbuf[POOL_BUF:POOL_BUF + tv, cols]
        s = u_g
        for back in range(1, w):
            s = s + poolbuf[POOL_BUF - back:POOL_BUF - back + tv, cols]
        cnt = jnp.minimum(pos + 1, w).astype(F32)
        pooled = (s / cnt - u_g).astype(BF16)
        po = jnp.dot(pooled, wpool_ref[g], preferred_element_type=F32)
        mixed_ref[:, cols] = (po * pscale_ref[:, cols] * gz_ref[:, cols]).astype(BF16)
    poolbuf[0:POOL_BUF, :] = poolbuf[tv:tv + POOL_BUF, :]

    x_dt = dtr_ref[...] + dtb_ref[...]
    dt = jnp.maximum(x_dt, 0.0) + jnp.log1p(jnp.exp(-jnp.abs(x_dt)))
    if tv < t:
        dt = jnp.concatenate([dt, jnp.zeros((t - tv, LANES), F32)], axis=0)
    a = dt * (-LOG2E * jnp.exp(alog_ref[...]))
    ri = lax.broadcasted_iota(jnp.int32, (t, t), 0)
    ci = lax.broadcasted_iota(jnp.int32, (t, t), 1)
    tril = ri >= ci
    tri = jnp.where(tril, 1.0, 0.0).astype(BF16)
    a_cs = sum(jnp.dot(tri, piece, preferred_element_type=F32) for piece in _split3(a))
    a_last = a_cs[t - 1:t, :]
    ea_last = jnp.exp2(a_last)
    a_cs_t = a_cs.T
    dt_t = dt.T
    w_t = (jnp.exp2(a_last - a_cs) * dt).T
    lo_half = lax.broadcasted_iota(jnp.int32, (1, LANES), 1) < HEAD_DIM

    def block_diag(v):
        zero = jnp.zeros_like(v)
        return jnp.concatenate([jnp.where(lo_half, v, zero), jnp.where(lo_half, zero, v)], axis=0)

    for g in range(GROUPS):
        b_g = pad_rows(xc_ref[:, W_SSD + g * STATE:W_SSD + (g + 1) * STATE])
        c_g = pad_rows(xc_ref[:, W_SSD + GROUPS * STATE + g * STATE:W_SSD + GROUPS * STATE + (g + 1) * STATE])
        cb = lax.dot_general(c_g.astype(BF16), b_g.astype(BF16), (((1,), (1,)), ((), ())),
                             preferred_element_type=F32)
        b_t = b_g.T
        ys = []
        for j in range(PAIRS_PER_GROUP):
            p = g * PAIRS_PER_GROUP + j
            pc = slice(p * LANES, (p + 1) * LANES)
            jc = slice(j * LANES, (j + 1) * LANES)
            xs_p = pad_rows(xc_ref[:, pc])
            h_p = h_ref[g, :, jc]
            bd_x = block_diag(xs_p.astype(BF16))
            bd_h = block_diag(h_p.astype(BF16))
            m_parts, e_parts, b_parts = [], [], []
            for k in (2 * p, 2 * p + 1):
                col = jnp.broadcast_to(a_cs[:, k:k + 1], (t, t))
                lmat = jnp.where(tril, jnp.exp2(col - a_cs_t[k:k + 1, :]), 0.0)
                m_parts.append((cb * lmat * dt_t[k:k + 1, :]).astype(BF16))
                e_parts.append((c_g * jnp.exp2(col)).astype(BF16))
                b_parts.append((b_t * w_t[k:k + 1, :]).astype(BF16))
            y_mm = jnp.dot(jnp.concatenate(m_parts + e_parts, axis=1),
                           jnp.concatenate([bd_x, bd_h], axis=0), preferred_element_type=F32)
            ys.append(y_mm + dskip_ref[:, pc] * xs_p)
            st = jnp.dot(jnp.concatenate(b_parts, axis=1), bd_x, preferred_element_type=F32)
            cd = jnp.where(lo_half, ea_last[:, 2 * p:2 * p + 1], ea_last[:, 2 * p + 1:2 * p + 2])
            h_ref[g, :, jc] = cd * h_p + st
        gc = slice(g * GROUP_W, (g + 1) * GROUP_W)
        y_g = jnp.concatenate(ys, axis=1)[0:tv, :]
        y_g = y_g * gz_ref[:, W_POOL + g * GROUP_W:W_POOL + (g + 1) * GROUP_W]
        ms = jnp.mean(y_g * y_g, axis=-1, keepdims=True)
        out = y_g * lax.rsqrt(ms + RMS_EPS) * normw_ref[:, gc]
        mixed_ref[:, W_POOL + g * GROUP_W:W_POOL + (g + 1) * GROUP_W] = out.astype(BF16)

    @pl.when(c == nchunks - 1)
    def _fin():
        npool_ref[...] = poolbuf[0:POOL_BUF, :]
        for g in range(GROUPS):
            for j in range(PAIRS_PER_GROUP):
                r0 = (g * HPG + 2 * j) * HEAD_DIM
                nssm_ref[r0:r0 + LANES, :] = h_ref[g, :, j * LANES:(j + 1) * LANES].T


def _mixer(u, gz, xc, dt_raw, init, params, layer, nb, seq, start_pos):
    tv = min(seq, MIX_T)
    nchunks = seq // tv
    has_init = init is not None
    row_map = lambda b, c: (b * nchunks + c, 0)
    per_layer2 = lambda b, c: (layer, 0, 0)
    per_b = lambda b, c: (b, 0, 0)
    per_layer_b = lambda b, c: (layer, b, 0, 0)
    in_specs = [pl.BlockSpec((tv, W_POOL), row_map), pl.BlockSpec((tv, W_POOL + W_SSD), row_map),
                pl.BlockSpec((tv, CONV_DIM), row_map), pl.BlockSpec((tv, LANES), row_map)]
    args = [u, gz, xc, dt_raw]
    if has_init:
        in_specs += [pl.BlockSpec((None, None, POOL_BUF, W_POOL), per_layer_b),
                     pl.BlockSpec((None, None, W_SSD, STATE), per_layer_b)]
        args += list(init)
    in_specs += [
        pl.BlockSpec((None, len(POOL_WINDOWS), POOL_GW, POOL_GW), lambda b, c: (layer, 0, 0, 0)),
        pl.BlockSpec((None, 1, W_POOL), per_layer2),
        pl.BlockSpec((None, 1, LANES), per_layer2),
        pl.BlockSpec((None, 1, LANES), per_layer2),
        pl.BlockSpec((None, 1, W_SSD), per_layer2),
        pl.BlockSpec((None, 1, W_SSD), per_layer2),
    ]
    args += list(params)
    return pl.pallas_call(
        functools.partial(_mixer_kernel, tv=tv, nchunks=nchunks, start_pos=start_pos, has_init=has_init),
        grid=(nb, nchunks),
        in_specs=in_specs,
        out_specs=[
            pl.BlockSpec((tv, D_INNER), row_map),
            pl.BlockSpec((None, POOL_BUF, W_POOL), per_b),
            pl.BlockSpec((None, W_SSD, STATE), per_b),
        ],
        out_shape=[
            jax.ShapeDtypeStruct((nb * seq, D_INNER), BF16),
            jax.ShapeDtypeStruct((nb, POOL_BUF, W_POOL), F32),
            jax.ShapeDtypeStruct((nb, W_SSD, STATE), F32),
        ],
        scratch_shapes=[
            pltpu.VMEM((POOL_BUF + tv, W_POOL), F32),
            pltpu.VMEM((GROUPS, STATE, GROUP_W), F32),
        ],
        compiler_params=pltpu.CompilerParams(
            dimension_semantics=("parallel", "arbitrary"), vmem_limit_bytes=VMEM_LIMIT),
        name="mixer_init" if has_init else "mixer_zero",
    )(*args)


def _pad_rows_front(a, rows):
    pad = rows - a.shape[-2]
    return jnp.pad(a, [(0, 0)] * (a.ndim - 2) + [(pad, 0), (0, 0)])


def _pad_lanes(a):
    return jnp.pad(a, [(0, 0)] * (a.ndim - 1) + [(0, LANES - a.shape[-1])])


def kernel(x_prompt, x_sample, state_pool, state_conv, state_ssm, c_prompt, c_sample,
           w_ada, b_ada, w_in, w_pool, pool_scale, conv_w, conv_b,
           dt_bias, a_log, d_skip, ssd_norm_w, w_out, ln_g, ln_b):
    nbp, seq_p, _ = x_prompt.shape
    nbs, seq_s, _ = x_sample.shape
    past_len = 1024

    n_c = nbp + nbs
    c_rows = -(-n_c // 8) * 8
    c_all = jnp.concatenate([c_prompt, c_sample, jnp.zeros((c_rows - n_c, D_MODEL), F32)], axis=0)
    mod = _ada(c_all, w_ada, b_ada)

    w_in_t = jnp.swapaxes(w_in, 1, 2)
    w_dt_b = _pad_lanes(w_in[:, :, MAIN_DIM:]).astype(BF16)
    w_out_b = w_out.astype(BF16)
    conv_b3 = conv_b[:, None]
    params = (w_pool.astype(BF16), pool_scale[:, None],
              _pad_lanes(dt_bias)[:, None], _pad_lanes(a_log)[:, None],
              jnp.repeat(d_skip, HEAD_DIM, axis=-1)[:, None], ssd_norm_w[:, None])
    conv0 = _pad_rows_front(state_conv, CONV_BUF)
    init = (_pad_rows_front(state_pool, POOL_BUF),
            state_ssm.reshape(DEPTH, nbs, W_SSD, STATE))

    def in_projections(h, l, tm, hist, n_seg, tiles_per_seq):
        u = _inproj(h, w_in_t, l, OFF_U, W_POOL, "raw", 512, 1024)
        gz = _inproj(h, w_in_t, l, OFF_G, W_POOL + W_SSD, "silu", 512, 1024)
        xc, nconv = _inproj(h, w_in_t, l, OFF_XBC, CONV_DIM, "conv", tm, 512,
                            conv=(conv_w, conv_b3, hist), n_seg=n_seg, tiles_per_seq=tiles_per_seq)
        return u, gz, xc, nconv

    xp = x_prompt.reshape(nbp * seq_p, D_MODEL)
    xs = x_sample.reshape(nbs * seq_s, D_MODEL)
    outs = {k: [] for k in ("pp", "pc", "psm", "sp", "sc", "ssm")}
    tm_p = 1024
    for l in range(DEPTH):
        shift, scale, gate = (mod[l, :, i * D_MODEL:(i + 1) * D_MODEL] for i in range(3))

        sc3, sh3, g3 = (v[0:nbp].reshape(nbp, 1, D_MODEL) for v in (scale, shift, gate))
        h, dt_raw = _lnmod(xp, sc3, sh3, w_dt_b, l, tm=512)
        u, gz, xc, nconv = in_projections(h, l, tm_p, None, 1, seq_p // tm_p)
        mixed, npool, nssm = _mixer(u, gz, xc, dt_raw, None, params, l, nbp, seq_p, 0)
        xp = _outproj(mixed, w_out_b, xp, g3, ln_g, ln_b, l, tm=512, tn=256)
        outs["pp"].append(npool[:, 1:])
        outs["pc"].append(nconv[:, CONV_BUF - (CONV_W - 1):])
        outs["psm"].append(nssm.reshape(nbp, HEADS, HEAD_DIM, STATE))

        sc3, sh3, g3 = (v[nbp:n_c].reshape(1, nbs, D_MODEL) for v in (scale, shift, gate))
        h, dt_raw = _lnmod(xs, sc3, sh3, w_dt_b, l, tm=nbs * seq_s)
        u, gz, xc, nconv = in_projections(h, l, nbs * seq_s, conv0, nbs, 1)
        mixed, npool, nssm = _mixer(u, gz, xc, dt_raw, init, params, l, nbs, seq_s, past_len)
        xs = _outproj(mixed, w_out_b, xs, g3, ln_g, ln_b, l, tm=nbs * seq_s, tn=256)
        outs["sp"].append(npool[:, 1:])
        outs["sc"].append(nconv[:, CONV_BUF - (CONV_W - 1):])
        outs["ssm"].append(nssm.reshape(nbs, HEADS, HEAD_DIM, STATE))

    return (xp.reshape(nbp, seq_p, D_MODEL), xs.reshape(nbs, seq_s, D_MODEL),
            jnp.stack(outs["pp"]), jnp.stack(outs["pc"]), jnp.stack(outs["psm"]),
            jnp.stack(outs["sp"]), jnp.stack(outs["sc"]), jnp.stack(outs["ssm"]))
```

```python
import functools
import math

import jax
import jax.numpy as jnp
from jax import lax
from jax.experimental import pallas as pl
from jax.experimental.pallas import tpu as pltpu

F32 = jnp.float32
BF16 = jnp.bfloat16

D_MODEL = 4096
DEPTH = 2
D_INNER = 2 * D_MODEL
W_POOL = D_INNER // 4
POOL_WINDOWS = (2, 4, 8, 16)
POOL_GW = W_POOL // len(POOL_WINDOWS)
POOL_HIST = max(POOL_WINDOWS) - 1
W_SSD = D_INNER - W_POOL
HEAD_DIM = 64
HEADS = W_SSD // HEAD_DIM
GROUPS = 8
HPG = HEADS // GROUPS
GROUP_W = HPG * HEAD_DIM
STATE = 128
CONV_W = 4
CONV_DIM = W_SSD + 2 * GROUPS * STATE
MAIN_DIM = 2 * W_POOL + W_SSD + CONV_DIM
ALPHA = (2 * DEPTH) ** 0.25
LN_EPS = 1e-5
RMS_EPS = 1e-5
LOG2E = math.log2(math.e)

LANES = 128
POOL_BUF = 16
CONV_BUF = 8
MIX_T = 128
PAIRS_PER_GROUP = GROUP_W // LANES
assert MIX_T == STATE == LANES and 2 * HEAD_DIM == LANES
VMEM_LIMIT = 56 * 1024 * 1024

OFF_U, OFF_G, OFF_Z, OFF_XBC = 0, W_POOL, 2 * W_POOL, 2 * W_POOL + W_SSD


def _silu_of_double(h):
    return h * jnp.tanh(h) + h


def _silu(x):
    return _silu_of_double(0.5 * x)


def _layer_norm(x):
    mu = jnp.mean(x, axis=-1, keepdims=True)
    xc = x - mu
    var = jnp.mean(xc * xc, axis=-1, keepdims=True)
    return xc * lax.rsqrt(var + LN_EPS)


def _ada_kernel(c_ref, w_ref, b_ref, o_ref):
    sc = _silu(c_ref[...]).astype(BF16)
    o_ref[...] = jnp.dot(sc, w_ref[...].astype(BF16), preferred_element_type=F32) + b_ref[...]


def _ada(c_all, w_ada, b_ada, tn=512):
    rows = c_all.shape[0]
    n = w_ada.shape[-1]
    return pl.pallas_call(
        _ada_kernel,
        grid=(DEPTH, n // tn),
        in_specs=[
            pl.BlockSpec((rows, D_MODEL), lambda l, j: (0, 0)),
            pl.BlockSpec((None, D_MODEL, tn), lambda l, j: (l, 0, j)),
            pl.BlockSpec((None, 1, tn), lambda l, j: (l, 0, j)),
        ],
        out_specs=pl.BlockSpec((None, rows, tn), lambda l, j: (l, 0, j)),
        out_shape=jax.ShapeDtypeStruct((DEPTH, rows, n), F32),
        compiler_params=pltpu.CompilerParams(
            dimension_semantics=("arbitrary", "arbitrary"), vmem_limit_bytes=VMEM_LIMIT),
        name="ada_mod",
    )(c_all, w_ada, b_ada.reshape(DEPTH, 1, n))


def _lnmod_kernel(x_ref, sc_ref, sh_ref, wdt_ref, h_ref, dt_ref, *, sub):
    n_mod = sc_ref.shape[0]
    rows_per_mod = x_ref.shape[0] // n_mod
    step = min(sub, rows_per_mod)
    for r in range(n_mod):
        for s0 in range(0, rows_per_mod, step):
            rs = slice(r * rows_per_mod + s0, r * rows_per_mod + s0 + step)
            h = _layer_norm(x_ref[rs, :]) * (1.0 + sc_ref[r:r + 1, :]) + sh_ref[r:r + 1, :]
            h_ref[rs, :] = h.astype(BF16)
    dt_ref[...] = jnp.dot(h_ref[...], wdt_ref[...], preferred_element_type=F32)


def _lnmod(x2d, scale3, shift3, w_dt_b, layer, tm):
    m = x2d.shape[0]
    n_mod = scale3.shape[1]
    tiles_per_mod_block = (m // tm) // scale3.shape[0]
    mod_spec = pl.BlockSpec((None, n_mod, D_MODEL), lambda i: (i // tiles_per_mod_block, 0, 0))
    return pl.pallas_call(
        functools.partial(_lnmod_kernel, sub=128),
        grid=(m // tm,),
        in_specs=[
            pl.BlockSpec((tm, D_MODEL), lambda i: (i, 0)),
            mod_spec,
            mod_spec,
            pl.BlockSpec((None, D_MODEL, LANES), lambda i: (layer, 0, 0)),
        ],
        out_specs=[
            pl.BlockSpec((tm, D_MODEL), lambda i: (i, 0)),
            pl.BlockSpec((tm, LANES), lambda i: (i, 0)),
        ],
        out_shape=[
            jax.ShapeDtypeStruct((m, D_MODEL), BF16),
            jax.ShapeDtypeStruct((m, LANES), F32),
        ],
        compiler_params=pltpu.CompilerParams(
            dimension_semantics=("parallel",), vmem_limit_bytes=VMEM_LIMIT),
        name="ln_mod",
    )(x2d, scale3, shift3, w_dt_b)


def _inproj_kernel(*refs, mode, n_seg, has_init, tiles_per_seq, sub):
    if mode == "conv":
        if has_init:
            h_ref, w_ref, cw_ref, cb_ref, hist_ref, o_ref, st_ref, wb_ref, buf = refs
        else:
            h_ref, w_ref, cw_ref, cb_ref, o_ref, st_ref, wb_ref, buf = refs
    else:
        h_ref, w_ref, o_ref, wb_ref = refs
    i = pl.program_id(1)

    @pl.when(i == 0)
    def _():
        wb_ref[...] = w_ref[...].astype(BF16)

    def project(r0, n):
        return lax.dot_general(h_ref[r0:r0 + n, :], wb_ref[...], (((1,), (1,)), ((), ())),
                               preferred_element_type=F32)

    tm = h_ref.shape[0]
    if mode == "raw":
        o_ref[...] = project(0, tm)
    elif mode == "silu":
        o_ref[...] = _silu(project(0, tm))
    else:
        seg = tm // n_seg
        half = tm // 2 if (n_seg == 1 and tm >= 2 * sub) else tm
        if not has_init:
            @pl.when(i % tiles_per_seq == 0)
            def _():
                buf[0, 0:CONV_BUF, :] = jnp.zeros((CONV_BUF, buf.shape[-1]), F32)
        for s in range(n_seg):
            if has_init:
                buf[s, 0:CONV_BUF, :] = hist_ref[s]
        cw_half = 0.5 * cw_ref[...]
        cb_half = 0.5 * cb_ref[...]
        for h0 in range(0, tm, half):
            res = project(h0, half)
            for s in range(h0 // seg, max(h0 // seg + 1, (h0 + half) // seg)):
                lo = max(h0, s * seg) - s * seg
                hi = min(h0 + half, (s + 1) * seg) - s * seg
                buf[s, CONV_BUF + lo:CONV_BUF + hi, :] = res[s * seg + lo - h0:s * seg + hi - h0, :]
                for r0 in range(lo, hi, sub):
                    n = min(sub, hi - r0)
                    acc = cb_half + buf[s, CONV_BUF + r0:CONV_BUF + r0 + n, :] * cw_half[CONV_W - 1:CONV_W, :]
                    for back in range(1, CONV_W):
                        acc = acc + (buf[s, CONV_BUF + r0 - back:CONV_BUF + r0 - back + n, :]
                                     * cw_half[CONV_W - 1 - back:CONV_W - back, :])
                    o_ref[s * seg + r0:s * seg + r0 + n, :] = _silu_of_double(acc)
        for s in range(n_seg):
            st_ref[s] = buf[s, seg:seg + CONV_BUF, :]
            if not has_init:
                buf[s, 0:CONV_BUF, :] = buf[s, seg:seg + CONV_BUF, :]


def _inproj(h2d, w_in_t, layer, col0, width, mode, tm, tn, conv=None, n_seg=1, tiles_per_seq=1):
    m = h2d.shape[0]
    j0 = col0 // tn
    in_specs = [
        pl.BlockSpec((tm, D_MODEL), lambda j, i: (i, 0)),
        pl.BlockSpec((None, tn, D_MODEL), lambda j, i: (layer, j0 + j, 0)),
    ]
    args = [h2d, w_in_t]
    out_specs = pl.BlockSpec((tm, tn), lambda j, i: (i, j))
    out_shape = jax.ShapeDtypeStruct((m, width), F32)
    scratch = [pltpu.VMEM((tn, D_MODEL), BF16)]
    has_init = False
    if mode == "conv":
        conv_w, conv_b, hist = conv
        has_init = hist is not None
        in_specs += [pl.BlockSpec((None, CONV_W, tn), lambda j, i: (layer, 0, j)),
                     pl.BlockSpec((None, 1, tn), lambda j, i: (layer, 0, j))]
        args += [conv_w, conv_b]
        if has_init:
            in_specs.append(pl.BlockSpec((None, n_seg, CONV_BUF, tn), lambda j, i: (layer, 0, 0, j)))
            args.append(hist)
        n_seq = (m // tm) * n_seg // tiles_per_seq
        out_specs = [out_specs,
                     pl.BlockSpec((n_seg, CONV_BUF, tn), lambda j, i: (i // tiles_per_seq, 0, j))]
        out_shape = [out_shape, jax.ShapeDtypeStruct((n_seq, CONV_BUF, width), F32)]
        scratch.append(pltpu.VMEM((n_seg, CONV_BUF + tm // n_seg, tn), F32))
    return pl.pallas_call(
        functools.partial(_inproj_kernel, mode=mode, n_seg=n_seg, has_init=has_init,
                          tiles_per_seq=tiles_per_seq, sub=256),
        grid=(width // tn, m // tm),
        in_specs=in_specs,
        out_specs=out_specs,
        out_shape=out_shape,
        scratch_shapes=scratch,
        compiler_params=pltpu.CompilerParams(
            dimension_semantics=("arbitrary", "arbitrary"), vmem_limit_bytes=VMEM_LIMIT),
        name="in_proj_" + mode,
    )(*args)


def _outproj_kernel(m_ref, w_ref, x_ref, g_ref, o_ref, wb_ref):
    @pl.when(pl.program_id(1) == 0)
    def _():
        wb_ref[...] = w_ref[...].astype(BF16)

    n_mod = g_ref.shape[0]
    rows_per_mod = m_ref.shape[0] // n_mod
    o = jnp.dot(m_ref[...], wb_ref[...], preferred_element_type=F32)
    for r in range(n_mod):
        rs = slice(r * rows_per_mod, (r + 1) * rows_per_mod)
        o_ref[rs, :] = ALPHA * x_ref[rs, :] + g_ref[r:r + 1, :] * o[rs, :]


def _outproj(mixed, w_out, x2d, gate3, layer, tm, tn):
    m = x2d.shape[0]
    n_mod = gate3.shape[1]
    tiles_per_mod_block = (m // tm) // gate3.shape[0]
    return pl.pallas_call(
        _outproj_kernel,
        grid=(D_MODEL // tn, m // tm),
        in_specs=[
            pl.BlockSpec((tm, D_INNER), lambda j, i: (i, 0)),
            pl.BlockSpec((None, D_INNER, tn), lambda j, i: (layer, 0, j)),
            pl.BlockSpec((tm, tn), lambda j, i: (i, j)),
            pl.BlockSpec((None, n_mod, tn), lambda j, i: (i // tiles_per_mod_block, 0, j)),
        ],
        out_specs=pl.BlockSpec((tm, tn), lambda j, i: (i, j)),
        out_shape=jax.ShapeDtypeStruct((m, D_MODEL), F32),
        scratch_shapes=[pltpu.VMEM((D_INNER, tn), BF16)],
        compiler_params=pltpu.CompilerParams(
            dimension_semantics=("arbitrary", "arbitrary"), vmem_limit_bytes=VMEM_LIMIT),
        name="out_proj",
    )(mixed, w_out, x2d, gate3)


def _postln_kernel(*refs, sub, emit_next):
    if emit_next:
        r_ref, lng_ref, lnb_ref, sc_ref, sh_ref, wdt_ref, x_ref, h_ref, dt_ref = refs
    else:
        r_ref, lng_ref, lnb_ref, x_ref = refs
    n_mod = sc_ref.shape[0] if emit_next else 1
    rows_per_mod = r_ref.shape[0] // n_mod
    step = min(sub, rows_per_mod)
    for r in range(n_mod):
        for s0 in range(0, rows_per_mod, step):
            rs = slice(r * rows_per_mod + s0, r * rows_per_mod + s0 + step)
            x_new = _layer_norm(r_ref[rs, :]) * lng_ref[...] + lnb_ref[...]
            x_ref[rs, :] = x_new
            if emit_next:
                h = _layer_norm(x_new) * (1.0 + sc_ref[r:r + 1, :]) + sh_ref[r:r + 1, :]
                h_ref[rs, :] = h.astype(BF16)
    if emit_next:
        dt_ref[...] = jnp.dot(h_ref[...], wdt_ref[...], preferred_element_type=F32)


def _postln(r2d, ln_g, ln_b, layer, tm, nxt=None):
    m = r2d.shape[0]
    emit_next = nxt is not None
    vec_spec = pl.BlockSpec((None, 1, D_MODEL), lambda i: (layer, 0, 0))
    row_spec = pl.BlockSpec((tm, D_MODEL), lambda i: (i, 0))
    in_specs = [row_spec, vec_spec, vec_spec]
    args = [r2d, ln_g.reshape(DEPTH, 1, D_MODEL), ln_b.reshape(DEPTH, 1, D_MODEL)]
    out_specs = [row_spec]
    out_shape = [jax.ShapeDtypeStruct((m, D_MODEL), F32)]
    if emit_next:
        scale3, shift3, w_dt_b, next_layer = nxt
        n_mod = scale3.shape[1]
        tiles_per_mod_block = (m // tm) // scale3.shape[0]
        mod_spec = pl.BlockSpec((None, n_mod, D_MODEL), lambda i: (i // tiles_per_mod_block, 0, 0))
        in_specs += [mod_spec, mod_spec, pl.BlockSpec((None, D_MODEL, LANES), lambda i: (next_layer, 0, 0))]
        args += [scale3, shift3, w_dt_b]
        out_specs += [row_spec, pl.BlockSpec((tm, LANES), lambda i: (i, 0))]
        out_shape += [jax.ShapeDtypeStruct((m, D_MODEL), BF16), jax.ShapeDtypeStruct((m, LANES), F32)]
    return pl.pallas_call(
        functools.partial(_postln_kernel, sub=128, emit_next=emit_next),
        grid=(m // tm,),
        in_specs=in_specs,
        out_specs=out_specs,
        out_shape=out_shape,
        compiler_params=pltpu.CompilerParams(
            dimension_semantics=("parallel",), vmem_limit_bytes=VMEM_LIMIT),
        name="post_ln",
    )(*args)


def _split3(a):
    hi = a.astype(BF16)
    r1 = a - hi.astype(F32)
    mid = r1.astype(BF16)
    lo = (r1 - mid.astype(F32)).astype(BF16)
    return hi, mid, lo


def _mixer_kernel(*refs, tv, nchunks, start_pos, has_init):
    if has_init:
        (u_ref, gz_ref, xc_ref, dtr_ref, pool0_ref, ssm0_ref, *rest) = refs
    else:
        (u_ref, gz_ref, xc_ref, dtr_ref, *rest) = refs
    (wpool_ref, pscale_ref, dtb_ref, alog_ref, dskip_ref, normw_ref,
     mixed_ref, npool_ref, nssm_ref, poolbuf, h_ref) = rest
    t = MIX_T
    c = pl.program_id(1)

    def pad_rows(v):
        if tv == t:
            return v
        return jnp.concatenate([v, jnp.zeros((t - tv, v.shape[1]), v.dtype)], axis=0)

    @pl.when(c == 0)
    def _init():
        if has_init:
            poolbuf[0:POOL_BUF, :] = pool0_ref[...]
            for g in range(GROUPS):
                for j in range(PAIRS_PER_GROUP):
                    r0 = (g * HPG + 2 * j) * HEAD_DIM
                    h_ref[g, :, j * LANES:(j + 1) * LANES] = ssm0_ref[r0:r0 + LANES, :].T
        else:
            poolbuf[0:POOL_BUF, :] = jnp.zeros((POOL_BUF, W_POOL), F32)
            h_ref[...] = jnp.zeros(h_ref.shape, F32)

    poolbuf[POOL_BUF:POOL_BUF + tv, :] = u_ref[...]
    row = lax.broadcasted_iota(jnp.int32, (tv, 1), 0)
    pos = start_pos + c * tv + row
    for g, w in enumerate(POOL_WINDOWS):
        cols = slice(g * POOL_GW, (g + 1) * POOL_GW)
        u_g = poolbuf[POOL_BUF:POOL_BUF + tv, cols]
        s = u_g
        for back in range(1, w):
            s = s + poolbuf[POOL_BUF - back:POOL_BUF - back + tv, cols]
        cnt = jnp.minimum(pos + 1, w).astype(F32)
        pooled = (s / cnt - u_g).astype(BF16)
        po = jnp.dot(pooled, wpool_ref[g], preferred_element_type=F32)
        mixed_ref[:, cols] = (po * pscale_ref[:, cols] * gz_ref[:, cols]).astype(BF16)
    poolbuf[0:POOL_BUF, :] = poolbuf[tv:tv + POOL_BUF, :]

    x_dt = dtr_ref[...] + dtb_ref[...]
    dt = jnp.maximum(x_dt, 0.0) + jnp.log1p(jnp.exp(-jnp.abs(x_dt)))
    if tv < t:
        dt = jnp.concatenate([dt, jnp.zeros((t - tv, LANES), F32)], axis=0)
    a = dt * (-LOG2E * jnp.exp(alog_ref[...]))
    ri = lax.broadcasted_iota(jnp.int32, (t, t), 0)
    ci = lax.broadcasted_iota(jnp.int32, (t, t), 1)
    tril = ri >= ci
    tri = jnp.where(tril, 1.0, 0.0).astype(BF16)
    a_cs = sum(jnp.dot(tri, piece, preferred_element_type=F32) for piece in _split3(a))
    a_last = a_cs[t - 1:t, :]
    ea_last = jnp.exp2(a_last)
    a_cs_t = a_cs.T
    dt_t = dt.T
    w_t = (jnp.exp2(a_last - a_cs) * dt).T
    lo_half = lax.broadcasted_iota(jnp.int32, (1, LANES), 1) < HEAD_DIM

    def block_diag(v):
        zero = jnp.zeros_like(v)
        return jnp.concatenate([jnp.where(lo_half, v, zero), jnp.where(lo_half, zero, v)], axis=0)

    for g in range(GROUPS):
        b_g = pad_rows(xc_ref[:, W_SSD + g * STATE:W_SSD + (g + 1) * STATE])
        c_g = pad_rows(xc_ref[:, W_SSD + GROUPS * STATE + g * STATE:W_SSD + GROUPS * STATE + (g + 1) * STATE])
        cb = lax.dot_general(c_g.astype(BF16), b_g.astype(BF16), (((1,), (1,)), ((), ())),
                             preferred_element_type=F32)
        b_t = b_g.T
        ys = []
        for j in range(PAIRS_PER_GROUP):
            p = g * PAIRS_PER_GROUP + j
            pc = slice(p * LANES, (p + 1) * LANES)
            jc = slice(j * LANES, (j + 1) * LANES)
            xs_p = pad_rows(xc_ref[:, pc])
            h_p = h_ref[g, :, jc]
            bd_x = block_diag(xs_p.astype(BF16))
            bd_h = block_diag(h_p.astype(BF16))
            m_parts, e_parts, b_parts = [], [], []
            for k in (2 * p, 2 * p + 1):
                col = jnp.broadcast_to(a_cs[0:tv, k:k + 1], (tv, t))
                lmat = jnp.where(tril[0:tv], jnp.exp2(col - a_cs_t[k:k + 1, :]), 0.0)
                m_parts.append((cb[0:tv] * lmat * dt_t[k:k + 1, :]).astype(BF16))
                e_parts.append((c_g[0:tv] * jnp.exp2(col)).astype(BF16))
                b_parts.append((b_t * w_t[k:k + 1, :]).astype(BF16))
            y_mm = jnp.dot(jnp.concatenate(m_parts + e_parts, axis=1),
                           jnp.concatenate([bd_x, bd_h], axis=0), preferred_element_type=F32)
            ys.append(y_mm + dskip_ref[:, pc] * xs_p[0:tv])
            st = jnp.dot(jnp.concatenate(b_parts, axis=1), bd_x, preferred_element_type=F32)
            cd = jnp.where(lo_half, ea_last[:, 2 * p:2 * p + 1], ea_last[:, 2 * p + 1:2 * p + 2])
            h_ref[g, :, jc] = cd * h_p + st
        gc = slice(g * GROUP_W, (g + 1) * GROUP_W)
        y_g = jnp.concatenate(ys, axis=1)
        y_g = y_g * gz_ref[:, W_POOL + g * GROUP_W:W_POOL + (g + 1) * GROUP_W]
        ms = jnp.mean(y_g * y_g, axis=-1, keepdims=True)
        out = y_g * lax.rsqrt(ms + RMS_EPS) * normw_ref[:, gc]
        mixed_ref[:, W_POOL + g * GROUP_W:W_POOL + (g + 1) * GROUP_W] = out.astype(BF16)

    @pl.when(c == nchunks - 1)
    def _fin():
        npool_ref[...] = poolbuf[0:POOL_BUF, :]
        for g in range(GROUPS):
            for j in range(PAIRS_PER_GROUP):
                r0 = (g * HPG + 2 * j) * HEAD_DIM
                nssm_ref[r0:r0 + LANES, :] = h_ref[g, :, j * LANES:(j + 1) * LANES].T


def _mixer(u, gz, xc, dt_raw, init, params, layer, nb, seq, start_pos):
    tv = min(seq, MIX_T)
    nchunks = seq // tv
    has_init = init is not None
    row_map = lambda b, c: (b * nchunks + c, 0)
    per_layer2 = lambda b, c: (layer, 0, 0)
    per_b = lambda b, c: (b, 0, 0)
    per_layer_b = lambda b, c: (layer, b, 0, 0)
    in_specs = [pl.BlockSpec((tv, W_POOL), row_map), pl.BlockSpec((tv, W_POOL + W_SSD), row_map),
                pl.BlockSpec((tv, CONV_DIM), row_map), pl.BlockSpec((tv, LANES), row_map)]
    args = [u, gz, xc, dt_raw]
    if has_init:
        in_specs += [pl.BlockSpec((None, None, POOL_BUF, W_POOL), per_layer_b),
                     pl.BlockSpec((None, None, W_SSD, STATE), per_layer_b)]
        args += list(init)
    in_specs += [
        pl.BlockSpec((None, len(POOL_WINDOWS), POOL_GW, POOL_GW), lambda b, c: (layer, 0, 0, 0)),
        pl.BlockSpec((None, 1, W_POOL), per_layer2),
        pl.BlockSpec((None, 1, LANES), per_layer2),
        pl.BlockSpec((None, 1, LANES), per_layer2),
        pl.BlockSpec((None, 1, W_SSD), per_layer2),
        pl.BlockSpec((None, 1, W_SSD), per_layer2),
    ]
    args += list(params)
    return pl.pallas_call(
        functools.partial(_mixer_kernel, tv=tv, nchunks=nchunks, start_pos=start_pos, has_init=has_init),
        grid=(nb, nchunks),
        in_specs=in_specs,
        out_specs=[
            pl.BlockSpec((tv, D_INNER), row_map),
            pl.BlockSpec((None, POOL_BUF, W_POOL), per_b),
            pl.BlockSpec((None, W_SSD, STATE), per_b),
        ],
        out_shape=[
            jax.ShapeDtypeStruct((nb * seq, D_INNER), BF16),
            jax.ShapeDtypeStruct((nb, POOL_BUF, W_POOL), F32),
            jax.ShapeDtypeStruct((nb, W_SSD, STATE), F32),
        ],
        scratch_shapes=[
            pltpu.VMEM((POOL_BUF + tv, W_POOL), F32),
            pltpu.VMEM((GROUPS, STATE, GROUP_W), F32),
        ],
        compiler_params=pltpu.CompilerParams(
            dimension_semantics=("parallel", "arbitrary"), vmem_limit_bytes=VMEM_LIMIT),
        name="mixer_init" if has_init else "mixer_zero",
    )(*args)


def _pad_rows_front(a, rows):
    pad = rows - a.shape[-2]
    return jnp.pad(a, [(0, 0)] * (a.ndim - 2) + [(pad, 0), (0, 0)])


def _pad_lanes(a):
    return jnp.pad(a, [(0, 0)] * (a.ndim - 1) + [(0, LANES - a.shape[-1])])


def kernel(x_prompt, x_sample, state_pool, state_conv, state_ssm, c_prompt, c_sample,
           w_ada, b_ada, w_in, w_pool, pool_scale, conv_w, conv_b,
           dt_bias, a_log, d_skip, ssd_norm_w, w_out, ln_g, ln_b):
    nbp, seq_p, _ = x_prompt.shape
    nbs, seq_s, _ = x_sample.shape
    past_len = 1024

    n_c = nbp + nbs
    c_rows = -(-n_c // 8) * 8
    c_all = jnp.concatenate([c_prompt, c_sample, jnp.zeros((c_rows - n_c, D_MODEL), F32)], axis=0)
    mod = _ada(c_all, w_ada, b_ada)

    w_in_t = jnp.swapaxes(w_in, 1, 2)
    w_dt_b = _pad_lanes(w_in[:, :, MAIN_DIM:]).astype(BF16)
    conv_b3 = conv_b[:, None]
    params = (w_pool.astype(BF16), pool_scale[:, None],
              _pad_lanes(dt_bias)[:, None], _pad_lanes(a_log)[:, None],
              jnp.repeat(d_skip, HEAD_DIM, axis=-1)[:, None], ssd_norm_w[:, None])
    conv0 = _pad_rows_front(state_conv, CONV_BUF)
    init = (_pad_rows_front(state_pool, POOL_BUF),
            state_ssm.reshape(DEPTH, nbs, W_SSD, STATE))

    def in_projections(h, l, tm, hist, n_seg, tiles_per_seq):
        u = _inproj(h, w_in_t, l, OFF_U, W_POOL, "raw", 512, 1024)
        gz = _inproj(h, w_in_t, l, OFF_G, W_POOL + W_SSD, "silu", 512, 1024)
        xc, nconv = _inproj(h, w_in_t, l, OFF_XBC, CONV_DIM, "conv", tm, 512,
                            conv=(conv_w, conv_b3, hist), n_seg=n_seg, tiles_per_seq=tiles_per_seq)
        return u, gz, xc, nconv

    xp = x_prompt.reshape(nbp * seq_p, D_MODEL)
    xs = x_sample.reshape(nbs * seq_s, D_MODEL)
    outs = {k: [] for k in ("pp", "pc", "psm", "sp", "sc", "ssm")}
    tm_p = 1024
    tm_s = nbs * seq_s

    def mods(l, group):
        rows = (slice(0, nbp), (nbp, 1, D_MODEL)) if group == "prompt" else (slice(nbp, n_c), (1, nbs, D_MODEL))
        return tuple(mod[l, rows[0], i * D_MODEL:(i + 1) * D_MODEL].reshape(rows[1]) for i in range(3))

    sh_p, sc_p, _ = mods(0, "prompt")
    sh_s, sc_s, _ = mods(0, "sample")
    h_p, dt_p = _lnmod(xp, sc_p, sh_p, w_dt_b, 0, tm=512)
    h_s, dt_s = _lnmod(xs, sc_s, sh_s, w_dt_b, 0, tm=tm_s)
    for l in range(DEPTH):
        last = l + 1 == DEPTH

        def next_mods(group):
            if last:
                return None
            sh_n, sc_n, _ = mods(l + 1, group)
            return (sc_n, sh_n, w_dt_b, l + 1)

        u, gz, xc, nconv = in_projections(h_p, l, tm_p, None, 1, seq_p // tm_p)
        mixed, npool, nssm = _mixer(u, gz, xc, dt_p, None, params, l, nbp, seq_p, 0)
        r = _outproj(mixed, w_out, xp, mods(l, "prompt")[2], l, tm=512, tn=256)
        post = _postln(r, ln_g, ln_b, l, 512, next_mods("prompt"))
        xp = post[0]
        if not last:
            h_p, dt_p = post[1], post[2]
        outs["pp"].append(npool[:, 1:])
        outs["pc"].append(nconv[:, CONV_BUF - (CONV_W - 1):])
        outs["psm"].append(nssm.reshape(nbp, HEADS, HEAD_DIM, STATE))

        u, gz, xc, nconv = in_projections(h_s, l, tm_s, conv0, nbs, 1)
        mixed, npool, nssm = _mixer(u, gz, xc, dt_s, init, params, l, nbs, seq_s, past_len)
        r = _outproj(mixed, w_out, xs, mods(l, "sample")[2], l, tm=tm_s, tn=256)
        post = _postln(r, ln_g, ln_b, l, tm_s, next_mods("sample"))
        xs = post[0]
        if not last:
            h_s, dt_s = post[1], post[2]
        outs["sp"].append(npool[:, 1:])
        outs["sc"].append(nconv[:, CONV_BUF - (CONV_W - 1):])
        outs["ssm"].append(nssm.reshape(nbs, HEADS, HEAD_DIM, STATE))

    return (xp.reshape(nbp, seq_p, D_MODEL), xs.reshape(nbs, seq_s, D_MODEL),
            jnp.stack(outs["pp"]), jnp.stack(outs["pc"]), jnp.stack(outs["psm"]),
            jnp.stack(outs["sp"]), jnp.stack(outs["sc"]), jnp.stack(outs["ssm"]))
```

```python
import functools
import math

import jax
import jax.numpy as jnp
from jax import lax
from jax.experimental import pallas as pl
from jax.experimental.pallas import tpu as pltpu

F32 = jnp.float32
BF16 = jnp.bfloat16

D_MODEL = 4096
DEPTH = 2
D_INNER = 2 * D_MODEL
W_POOL = D_INNER // 4
POOL_WINDOWS = (2, 4, 8, 16)
POOL_GW = W_POOL // len(POOL_WINDOWS)
POOL_HIST = max(POOL_WINDOWS) - 1
W_SSD = D_INNER - W_POOL
HEAD_DIM = 64
HEADS = W_SSD // HEAD_DIM
GROUPS = 8
HPG = HEADS // GROUPS
GROUP_W = HPG * HEAD_DIM
STATE = 128
CONV_W = 4
CONV_DIM = W_SSD + 2 * GROUPS * STATE
MAIN_DIM = 2 * W_POOL + W_SSD + CONV_DIM
ALPHA = (2 * DEPTH) ** 0.25
LN_EPS = 1e-5
RMS_EPS = 1e-5
LOG2E = math.log2(math.e)

LANES = 128
POOL_BUF = 16
CONV_BUF = 8
MIX_T = 128
PAIRS_PER_GROUP = GROUP_W // LANES
assert MIX_T == STATE == LANES and 2 * HEAD_DIM == LANES
VMEM_LIMIT = 56 * 1024 * 1024

OFF_U, OFF_G, OFF_Z, OFF_XBC = 0, W_POOL, 2 * W_POOL, 2 * W_POOL + W_SSD


def _silu_of_double(h):
    return h * jnp.tanh(h) + h


def _silu(x):
    return _silu_of_double(0.5 * x)


def _layer_norm(x):
    mu = jnp.mean(x, axis=-1, keepdims=True)
    xc = x - mu
    var = jnp.mean(xc * xc, axis=-1, keepdims=True)
    return xc * lax.rsqrt(var + LN_EPS)


def _ada_kernel(c_ref, w_ref, b_ref, o_ref):
    sc = _silu(c_ref[...]).astype(BF16)
    o_ref[...] = jnp.dot(sc, w_ref[...].astype(BF16), preferred_element_type=F32) + b_ref[...]


def _ada(c_all, w_ada, b_ada, tn=512):
    rows = c_all.shape[0]
    n = w_ada.shape[-1]
    return pl.pallas_call(
        _ada_kernel,
        grid=(DEPTH, n // tn),
        in_specs=[
            pl.BlockSpec((rows, D_MODEL), lambda l, j: (0, 0)),
            pl.BlockSpec((None, D_MODEL, tn), lambda l, j: (l, 0, j)),
            pl.BlockSpec((None, 1, tn), lambda l, j: (l, 0, j)),
        ],
        out_specs=pl.BlockSpec((None, rows, tn), lambda l, j: (l, 0, j)),
        out_shape=jax.ShapeDtypeStruct((DEPTH, rows, n), F32),
        compiler_params=pltpu.CompilerParams(
            dimension_semantics=("arbitrary", "arbitrary"), vmem_limit_bytes=VMEM_LIMIT),
        name="ada_mod",
    )(c_all, w_ada, b_ada.reshape(DEPTH, 1, n))


def _lnmod_kernel(x_ref, sc_ref, sh_ref, wdt_ref, h_ref, dt_ref, *, sub):
    n_mod = sc_ref.shape[0]
    rows_per_mod = x_ref.shape[0] // n_mod
    step = min(sub, rows_per_mod)
    for r in range(n_mod):
        for s0 in range(0, rows_per_mod, step):
            rs = slice(r * rows_per_mod + s0, r * rows_per_mod + s0 + step)
            h = _layer_norm(x_ref[rs, :]) * (1.0 + sc_ref[r:r + 1, :]) + sh_ref[r:r + 1, :]
            h_ref[rs, :] = h.astype(BF16)
    dt_ref[...] = jnp.dot(h_ref[...], wdt_ref[...], preferred_element_type=F32)


def _lnmod(x2d, scale3, shift3, w_dt_b, layer, tm):
    m = x2d.shape[0]
    n_mod = scale3.shape[1]
    tiles_per_mod_block = (m // tm) // scale3.shape[0]
    mod_spec = pl.BlockSpec((None, n_mod, D_MODEL), lambda i: (i // tiles_per_mod_block, 0, 0))
    return pl.pallas_call(
        functools.partial(_lnmod_kernel, sub=128),
        grid=(m // tm,),
        in_specs=[
            pl.BlockSpec((tm, D_MODEL), lambda i: (i, 0)),
            mod_spec,
            mod_spec,
            pl.BlockSpec((None, D_MODEL, LANES), lambda i: (layer, 0, 0)),
        ],
        out_specs=[
            pl.BlockSpec((tm, D_MODEL), lambda i: (i, 0)),
            pl.BlockSpec((tm, LANES), lambda i: (i, 0)),
        ],
        out_shape=[
            jax.ShapeDtypeStruct((m, D_MODEL), BF16),
            jax.ShapeDtypeStruct((m, LANES), F32),
        ],
        compiler_params=pltpu.CompilerParams(
            dimension_semantics=("parallel",), vmem_limit_bytes=VMEM_LIMIT),
        name="ln_mod",
    )(x2d, scale3, shift3, w_dt_b)


def _inproj_kernel(*refs, mode, n_seg, has_init, tiles_per_seq, sub):
    if mode == "conv":
        if has_init:
            h_ref, w_ref, cw_ref, cb_ref, hist_ref, o_ref, st_ref, wb_ref, buf = refs
        else:
            h_ref, w_ref, cw_ref, cb_ref, o_ref, st_ref, wb_ref, buf = refs
    else:
        h_ref, w_ref, o_ref, wb_ref = refs
    i = pl.program_id(1)

    @pl.when(i == 0)
    def _():
        wb_ref[...] = w_ref[...].astype(BF16)

    def project(r0, n):
        return lax.dot_general(h_ref[r0:r0 + n, :], wb_ref[...], (((1,), (1,)), ((), ())),
                               preferred_element_type=F32)

    tm = h_ref.shape[0]
    if mode == "raw":
        o_ref[...] = project(0, tm)
    elif mode == "silu":
        o_ref[...] = _silu(project(0, tm))
    else:
        seg = tm // n_seg
        half = tm // 2 if (n_seg == 1 and tm >= 2 * sub) else tm
        if not has_init:
            @pl.when(i % tiles_per_seq == 0)
            def _():
                buf[0, 0:CONV_BUF, :] = jnp.zeros((CONV_BUF, buf.shape[-1]), F32)
        for s in range(n_seg):
            if has_init:
                buf[s, 0:CONV_BUF, :] = hist_ref[s]
        cw_half = 0.5 * cw_ref[...]
        cb_half = 0.5 * cb_ref[...]
        for h0 in range(0, tm, half):
            res = project(h0, half)
            for s in range(h0 // seg, max(h0 // seg + 1, (h0 + half) // seg)):
                lo = max(h0, s * seg) - s * seg
                hi = min(h0 + half, (s + 1) * seg) - s * seg
                buf[s, CONV_BUF + lo:CONV_BUF + hi, :] = res[s * seg + lo - h0:s * seg + hi - h0, :]
                for r0 in range(lo, hi, sub):
                    n = min(sub, hi - r0)
                    acc = cb_half + buf[s, CONV_BUF + r0:CONV_BUF + r0 + n, :] * cw_half[CONV_W - 1:CONV_W, :]
                    for back in range(1, CONV_W):
                        acc = acc + (buf[s, CONV_BUF + r0 - back:CONV_BUF + r0 - back + n, :]
                                     * cw_half[CONV_W - 1 - back:CONV_W - back, :])
                    o_ref[s * seg + r0:s * seg + r0 + n, :] = _silu_of_double(acc)
        for s in range(n_seg):
            st_ref[s] = buf[s, seg:seg + CONV_BUF, :]
            if not has_init:
                buf[s, 0:CONV_BUF, :] = buf[s, seg:seg + CONV_BUF, :]


def _inproj(h2d, w_in_t, layer, col0, width, mode, tm, tn, conv=None, n_seg=1, tiles_per_seq=1):
    m = h2d.shape[0]
    j0 = col0 // tn
    in_specs = [
        pl.BlockSpec((tm, D_MODEL), lambda j, i: (i, 0)),
        pl.BlockSpec((None, tn, D_MODEL), lambda j, i: (layer, j0 + j, 0)),
    ]
    args = [h2d, w_in_t]
    out_specs = pl.BlockSpec((tm, tn), lambda j, i: (i, j))
    out_shape = jax.ShapeDtypeStruct((m, width), F32)
    scratch = [pltpu.VMEM((tn, D_MODEL), BF16)]
    has_init = False
    if mode == "conv":
        conv_w, conv_b, hist = conv
        has_init = hist is not None
        in_specs += [pl.BlockSpec((None, CONV_W, tn), lambda j, i: (layer, 0, j)),
                     pl.BlockSpec((None, 1, tn), lambda j, i: (layer, 0, j))]
        args += [conv_w, conv_b]
        if has_init:
            in_specs.append(pl.BlockSpec((None, n_seg, CONV_BUF, tn), lambda j, i: (layer, 0, 0, j)))
            args.append(hist)
        n_seq = (m // tm) * n_seg // tiles_per_seq
        out_specs = [out_specs,
                     pl.BlockSpec((n_seg, CONV_BUF, tn), lambda j, i: (i // tiles_per_seq, 0, j))]
        out_shape = [out_shape, jax.ShapeDtypeStruct((n_seq, CONV_BUF, width), F32)]
        scratch.append(pltpu.VMEM((n_seg, CONV_BUF + tm // n_seg, tn), F32))
    return pl.pallas_call(
        functools.partial(_inproj_kernel, mode=mode, n_seg=n_seg, has_init=has_init,
                          tiles_per_seq=tiles_per_seq, sub=256),
        grid=(width // tn, m // tm),
        in_specs=in_specs,
        out_specs=out_specs,
        out_shape=out_shape,
        scratch_shapes=scratch,
        compiler_params=pltpu.CompilerParams(
            dimension_semantics=("arbitrary", "arbitrary"), vmem_limit_bytes=VMEM_LIMIT),
        name="in_proj_" + mode,
    )(*args)


def _outproj_kernel(m_ref, w_ref, x_ref, g_ref, lng_ref, lnb_ref, o_ref, acc_ref, *, nj, tn, sub):
    j = pl.program_id(1)
    n_mod = g_ref.shape[0]
    tm = m_ref.shape[0]
    rows_per_mod = tm // n_mod
    o = jnp.dot(m_ref[...], w_ref[...], preferred_element_type=F32)
    for r in range(n_mod):
        rs = slice(r * rows_per_mod, (r + 1) * rows_per_mod)
        acc_ref[j, rs, :] = ALPHA * x_ref[rs, :] + g_ref[r:r + 1, :] * o[rs, :]

    @pl.when(j == nj - 1)
    def _():
        inv_d = 1.0 / (nj * tn)
        for s0 in range(0, tm, sub):
            rs = slice(s0, s0 + sub)
            total = acc_ref[0, rs, :].sum(axis=-1, keepdims=True)
            for jj in range(1, nj):
                total = total + acc_ref[jj, rs, :].sum(axis=-1, keepdims=True)
            mu = total * inv_d
            sq = jnp.zeros_like(mu)
            for jj in range(nj):
                d = acc_ref[jj, rs, :] - mu
                sq = sq + (d * d).sum(axis=-1, keepdims=True)
            inv = lax.rsqrt(sq * inv_d + LN_EPS)
            for jj in range(nj):
                cs = slice(jj * tn, (jj + 1) * tn)
                o_ref[rs, cs] = (acc_ref[jj, rs, :] - mu) * inv * lng_ref[:, cs] + lnb_ref[:, cs]


def _outproj(mixed, w_out_b, x2d, gate3, ln_g, ln_b, layer, tm, tn):
    m = x2d.shape[0]
    n_mod = gate3.shape[1]
    tiles_per_mod_block = (m // tm) // gate3.shape[0]
    nj = D_MODEL // tn
    vec_spec = pl.BlockSpec((None, 1, D_MODEL), lambda i, j: (layer, 0, 0))
    return pl.pallas_call(
        functools.partial(_outproj_kernel, nj=nj, tn=tn, sub=min(128, tm)),
        grid=(m // tm, nj),
        in_specs=[
            pl.BlockSpec((tm, D_INNER), lambda i, j: (i, 0)),
            pl.BlockSpec((None, D_INNER, tn), lambda i, j: (layer, 0, j)),
            pl.BlockSpec((tm, tn), lambda i, j: (i, j)),
            pl.BlockSpec((None, n_mod, tn), lambda i, j: (i // tiles_per_mod_block, 0, j)),
            vec_spec,
            vec_spec,
        ],
        out_specs=pl.BlockSpec((tm, D_MODEL), lambda i, j: (i, 0)),
        out_shape=jax.ShapeDtypeStruct((m, D_MODEL), F32),
        scratch_shapes=[pltpu.VMEM((nj, tm, tn), F32)],
        compiler_params=pltpu.CompilerParams(
            dimension_semantics=("parallel", "arbitrary"), vmem_limit_bytes=VMEM_LIMIT),
        name="out_proj",
    )(mixed, w_out_b, x2d, gate3, ln_g.reshape(DEPTH, 1, D_MODEL), ln_b.reshape(DEPTH, 1, D_MODEL))


def _split3(a):
    hi = a.astype(BF16)
    r1 = a - hi.astype(F32)
    mid = r1.astype(BF16)
    lo = (r1 - mid.astype(F32)).astype(BF16)
    return hi, mid, lo


def _mixer_kernel(*refs, tv, nchunks, start_pos, has_init, has_stack, layer):
    if has_init:
        (u_ref, gz_ref, xc_ref, dtr_ref, pool0_ref, ssm0_ref, *rest) = refs
    else:
        (u_ref, gz_ref, xc_ref, dtr_ref, *rest) = refs
    (wpool_ref, pscale_ref, dtb_ref, alog_ref, dskip_ref, normw_ref, *rest) = rest
    if has_stack:
        rest = rest[1:]
    mixed_ref, npool_ref, nssm_ref, poolbuf, winbuf, h_ref = rest
    t = MIX_T
    c = pl.program_id(1)

    def pad_rows(v):
        if tv == t:
            return v
        return jnp.concatenate([v, jnp.zeros((t - tv, v.shape[1]), v.dtype)], axis=0)

    @pl.when(c == 0)
    def _init():
        if has_init:
            poolbuf[0:POOL_BUF, :] = pool0_ref[...]
            for g in range(GROUPS):
                for j in range(PAIRS_PER_GROUP):
                    r0 = (g * HPG + 2 * j) * HEAD_DIM
                    h_ref[g, :, j * LANES:(j + 1) * LANES] = ssm0_ref[r0:r0 + LANES, :].T
        else:
            poolbuf[0:POOL_BUF, :] = jnp.zeros((POOL_BUF, W_POOL), F32)
            h_ref[...] = jnp.zeros(h_ref.shape, F32)

    poolbuf[POOL_BUF:POOL_BUF + tv, :] = u_ref[...]
    row = lax.broadcasted_iota(jnp.int32, (tv, 1), 0)
    pos = start_pos + c * tv + row
    for g, w in enumerate(POOL_WINDOWS):
        cols = slice(g * POOL_GW, (g + 1) * POOL_GW)
        u_g = poolbuf[POOL_BUF:POOL_BUF + tv, cols]
        src, d, level = poolbuf, 1, 0
        while 2 * d < w:
            lo = POOL_BUF - (w - 2 * d)
            dst = winbuf.at[level % 2]
            dst[lo:POOL_BUF + tv, cols] = src[lo:POOL_BUF + tv, cols] + src[lo - d:POOL_BUF + tv - d, cols]
            src, d, level = dst, 2 * d, level + 1
        s = src[POOL_BUF:POOL_BUF + tv, cols] + src[POOL_BUF - d:POOL_BUF + tv - d, cols]
        cnt = jnp.minimum(pos + 1, w).astype(F32)
        pooled = (s / cnt - u_g).astype(BF16)
        po = jnp.dot(pooled, wpool_ref[g], preferred_element_type=F32)
        mixed_ref[:, cols] = (po * pscale_ref[:, cols] * gz_ref[:, cols]).astype(BF16)
    poolbuf[0:POOL_BUF, :] = poolbuf[tv:tv + POOL_BUF, :]

    x_dt = dtr_ref[...] + dtb_ref[...]
    dt = jnp.maximum(x_dt, 0.0) + jnp.log1p(jnp.exp(-jnp.abs(x_dt)))
    if tv < t:
        dt = jnp.concatenate([dt, jnp.zeros((t - tv, LANES), F32)], axis=0)
    a = dt * (-LOG2E * jnp.exp(alog_ref[...]))
    ri = lax.broadcasted_iota(jnp.int32, (t, t), 0)
    ci = lax.broadcasted_iota(jnp.int32, (t, t), 1)
    tril = ri >= ci
    tri = jnp.where(tril, 1.0, 0.0).astype(BF16)
    a_cs = sum(jnp.dot(tri, piece, preferred_element_type=F32) for piece in _split3(a))
    a_last = a_cs[t - 1:t, :]
    ea_last = jnp.exp2(a_last)
    a_cs_t = a_cs.T
    dt_t = dt.T
    w_t = (jnp.exp2(a_last - a_cs) * dt).T
    lo_half = lax.broadcasted_iota(jnp.int32, (1, LANES), 1) < HEAD_DIM

    def block_diag(v):
        zero = jnp.zeros_like(v)
        return jnp.concatenate([jnp.where(lo_half, v, zero), jnp.where(lo_half, zero, v)], axis=0)

    for g in range(GROUPS):
        b_g = pad_rows(xc_ref[:, W_SSD + g * STATE:W_SSD + (g + 1) * STATE])
        c_g = pad_rows(xc_ref[:, W_SSD + GROUPS * STATE + g * STATE:W_SSD + GROUPS * STATE + (g + 1) * STATE])
        cb = lax.dot_general(c_g.astype(BF16), b_g.astype(BF16), (((1,), (1,)), ((), ())),
                             preferred_element_type=F32)
        b_t = b_g.T
        ys = []
        for j in range(PAIRS_PER_GROUP):
            p = g * PAIRS_PER_GROUP + j
            pc = slice(p * LANES, (p + 1) * LANES)
            jc = slice(j * LANES, (j + 1) * LANES)
            xs_p = pad_rows(xc_ref[:, pc])
            h_p = h_ref[g, :, jc]
            bd_x = block_diag(xs_p.astype(BF16))
            bd_h = block_diag(h_p.astype(BF16))
            m_parts, e_parts, b_parts = [], [], []
            for k in (2 * p, 2 * p + 1):
                col = jnp.broadcast_to(a_cs[0:tv, k:k + 1], (tv, t))
                lmat = jnp.where(tril[0:tv], jnp.exp2(col - a_cs_t[k:k + 1, :]), 0.0)
                m_parts.append((cb[0:tv] * lmat * dt_t[k:k + 1, :]).astype(BF16))
                e_parts.append((c_g[0:tv] * jnp.exp2(col)).astype(BF16))
                b_parts.append((b_t * w_t[k:k + 1, :]).astype(BF16))
            y_mm = jnp.dot(jnp.concatenate(m_parts + e_parts, axis=1),
                           jnp.concatenate([bd_x, bd_h], axis=0), preferred_element_type=F32)
            ys.append(y_mm + dskip_ref[:, pc] * xs_p[0:tv])
            st = jnp.dot(jnp.concatenate(b_parts, axis=1), bd_x, preferred_element_type=F32)
            cd = jnp.where(lo_half, ea_last[:, 2 * p:2 * p + 1], ea_last[:, 2 * p + 1:2 * p + 2])
            h_ref[g, :, jc] = cd * h_p + st
        gc = slice(g * GROUP_W, (g + 1) * GROUP_W)
        y_g = jnp.concatenate(ys, axis=1)
        y_g = y_g * gz_ref[:, W_POOL + g * GROUP_W:W_POOL + (g + 1) * GROUP_W]
        ms = jnp.mean(y_g * y_g, axis=-1, keepdims=True)
        out = y_g * lax.rsqrt(ms + RMS_EPS) * normw_ref[:, gc]
        mixed_ref[:, W_POOL + g * GROUP_W:W_POOL + (g + 1) * GROUP_W] = out.astype(BF16)

    @pl.when(c == nchunks - 1)
    def _fin():
        npool_ref[...] = poolbuf[0:POOL_BUF, :]
        if has_stack:
            state_out = nssm_ref
        else:
            state_out = nssm_ref.at[layer]
            for other in range(DEPTH):
                if other != layer:
                    nssm_ref[other] = jnp.zeros((W_SSD, STATE), F32)
        for g in range(GROUPS):
            for j in range(PAIRS_PER_GROUP):
                r0 = (g * HPG + 2 * j) * HEAD_DIM
                state_out[r0:r0 + LANES, :] = h_ref[g, :, j * LANES:(j + 1) * LANES].T


def _mixer(u, gz, xc, dt_raw, init, ssm_stack, params, layer, nb, seq, start_pos):
    tv = min(seq, MIX_T)
    nchunks = seq // tv
    has_init = init is not None
    has_stack = ssm_stack is not None
    row_map = lambda b, c: (b * nchunks + c, 0)
    per_layer2 = lambda b, c: (layer, 0, 0)
    per_b = lambda b, c: (b, 0, 0)
    per_layer_b = lambda b, c: (layer, b, 0, 0)
    in_specs = [pl.BlockSpec((tv, W_POOL), row_map), pl.BlockSpec((tv, W_POOL + W_SSD), row_map),
                pl.BlockSpec((tv, CONV_DIM), row_map), pl.BlockSpec((tv, LANES), row_map)]
    args = [u, gz, xc, dt_raw]
    if has_init:
        in_specs += [pl.BlockSpec((None, None, POOL_BUF, W_POOL), per_layer_b),
                     pl.BlockSpec((None, None, W_SSD, STATE), per_layer_b)]
        args += list(init)
    in_specs += [
        pl.BlockSpec((None, len(POOL_WINDOWS), POOL_GW, POOL_GW), lambda b, c: (layer, 0, 0, 0)),
        pl.BlockSpec((None, 1, W_POOL), per_layer2),
        pl.BlockSpec((None, 1, LANES), per_layer2),
        pl.BlockSpec((None, 1, LANES), per_layer2),
        pl.BlockSpec((None, 1, W_SSD), per_layer2),
        pl.BlockSpec((None, 1, W_SSD), per_layer2),
    ]
    args += list(params)
    aliases = {}
    if has_stack:
        in_specs.append(pl.BlockSpec(memory_space=pl.ANY))
        args.append(ssm_stack)
        aliases = {len(args) - 1: 2}
    return pl.pallas_call(
        functools.partial(_mixer_kernel, tv=tv, nchunks=nchunks, start_pos=start_pos, has_init=has_init,
                          has_stack=has_stack, layer=layer),
        grid=(nb, nchunks),
        in_specs=in_specs,
        out_specs=[
            pl.BlockSpec((tv, D_INNER), row_map),
            pl.BlockSpec((None, POOL_BUF, W_POOL), per_b),
            (pl.BlockSpec((None, None, W_SSD, STATE), per_layer_b) if has_stack
             else pl.BlockSpec((DEPTH, None, W_SSD, STATE), lambda b, c: (0, b, 0, 0))),
        ],
        out_shape=[
            jax.ShapeDtypeStruct((nb * seq, D_INNER), BF16),
            jax.ShapeDtypeStruct((nb, POOL_BUF, W_POOL), F32),
            jax.ShapeDtypeStruct((DEPTH, nb, W_SSD, STATE), F32),
        ],
        input_output_aliases=aliases,
        scratch_shapes=[
            pltpu.VMEM((POOL_BUF + tv, W_POOL), F32),
            pltpu.VMEM((2, POOL_BUF + tv, W_POOL), F32),
            pltpu.VMEM((GROUPS, STATE, GROUP_W), F32),
        ],
        compiler_params=pltpu.CompilerParams(
            dimension_semantics=("parallel", "arbitrary"), vmem_limit_bytes=VMEM_LIMIT),
        name="mixer_init" if has_init else "mixer_zero",
    )(*args)


def _pad_rows_front(a, rows):
    pad = rows - a.shape[-2]
    return jnp.pad(a, [(0, 0)] * (a.ndim - 2) + [(pad, 0), (0, 0)])


def _pad_lanes(a):
    return jnp.pad(a, [(0, 0)] * (a.ndim - 1) + [(0, LANES - a.shape[-1])])


def kernel(x_prompt, x_sample, state_pool, state_conv, state_ssm, c_prompt, c_sample,
           w_ada, b_ada, w_in, w_pool, pool_scale, conv_w, conv_b,
           dt_bias, a_log, d_skip, ssd_norm_w, w_out, ln_g, ln_b):
    nbp, seq_p, _ = x_prompt.shape
    nbs, seq_s, _ = x_sample.shape
    past_len = 1024

    n_c = nbp + nbs
    c_rows = -(-n_c // 8) * 8
    c_all = jnp.concatenate([c_prompt, c_sample, jnp.zeros((c_rows - n_c, D_MODEL), F32)], axis=0)
    mod = _ada(c_all, w_ada, b_ada)

    w_in_t = jnp.swapaxes(w_in, 1, 2)
    w_dt_b = _pad_lanes(w_in[:, :, MAIN_DIM:]).astype(BF16)
    w_out_b = w_out.astype(BF16)
    conv_b3 = conv_b[:, None]
    params = (w_pool.astype(BF16), pool_scale[:, None],
              _pad_lanes(dt_bias)[:, None], _pad_lanes(a_log)[:, None],
              jnp.repeat(d_skip, HEAD_DIM, axis=-1)[:, None], ssd_norm_w[:, None])
    conv0 = _pad_rows_front(state_conv, CONV_BUF)
    init = (_pad_rows_front(state_pool, POOL_BUF),
            state_ssm.reshape(DEPTH, nbs, W_SSD, STATE))

    def in_projections(h, l, tm, hist, n_seg, tiles_per_seq):
        u = _inproj(h, w_in_t, l, OFF_U, W_POOL, "raw", 512, 1024)
        gz = _inproj(h, w_in_t, l, OFF_G, W_POOL + W_SSD, "silu", 512, 1024)
        xc, nconv = _inproj(h, w_in_t, l, OFF_XBC, CONV_DIM, "conv", tm, 512,
                            conv=(conv_w, conv_b3, hist), n_seg=n_seg, tiles_per_seq=tiles_per_seq)
        return u, gz, xc, nconv

    xp = x_prompt.reshape(nbp * seq_p, D_MODEL)
    xs = x_sample.reshape(nbs * seq_s, D_MODEL)
    outs = {k: [] for k in ("pp", "pc", "psm", "sp", "sc", "ssm")}
    tm_p = 1024
    ssm_p = ssm_s = None
    for l in range(DEPTH):
        shift, scale, gate = (mod[l, :, i * D_MODEL:(i + 1) * D_MODEL] for i in range(3))

        sc3, sh3, g3 = (v[0:nbp].reshape(nbp, 1, D_MODEL) for v in (scale, shift, gate))
        h, dt_raw = _lnmod(xp, sc3, sh3, w_dt_b, l, tm=512)
        u, gz, xc, nconv = in_projections(h, l, tm_p, None, 1, seq_p // tm_p)
        mixed, npool, ssm_p = _mixer(u, gz, xc, dt_raw, None, ssm_p, params, l, nbp, seq_p, 0)
        xp = _outproj(mixed, w_out_b, xp, g3, ln_g, ln_b, l, tm=512, tn=256)
        outs["pp"].append(npool[:, 1:])
        outs["pc"].append(nconv[:, CONV_BUF - (CONV_W - 1):])

        sc3, sh3, g3 = (v[nbp:n_c].reshape(1, nbs, D_MODEL) for v in (scale, shift, gate))
        h, dt_raw = _lnmod(xs, sc3, sh3, w_dt_b, l, tm=nbs * seq_s)
        u, gz, xc, nconv = in_projections(h, l, nbs * seq_s, conv0, nbs, 1)
        mixed, npool, ssm_s = _mixer(u, gz, xc, dt_raw, init, ssm_s, params, l, nbs, seq_s, past_len)
        xs = _outproj(mixed, w_out_b, xs, g3, ln_g, ln_b, l, tm=nbs * seq_s, tn=256)
        outs["sp"].append(npool[:, 1:])
        outs["sc"].append(nconv[:, CONV_BUF - (CONV_W - 1):])

    return (xp.reshape(nbp, seq_p, D_MODEL), xs.reshape(nbs, seq_s, D_MODEL),
            jnp.stack(outs["pp"]), jnp.stack(outs["pc"]), ssm_p.reshape(DEPTH, nbp, HEADS, HEAD_DIM, STATE),
            jnp.stack(outs["sp"]), jnp.stack(outs["sc"]), ssm_s.reshape(DEPTH, nbs, HEADS, HEAD_DIM, STATE))
```

```python
import functools
import math

import jax
import jax.numpy as jnp
from jax import lax
from jax.experimental import pallas as pl
from jax.experimental.pallas import tpu as pltpu

F32 = jnp.float32
BF16 = jnp.bfloat16

D_MODEL = 4096
DEPTH = 2
D_INNER = 2 * D_MODEL
W_POOL = D_INNER // 4
POOL_WINDOWS = (2, 4, 8, 16)
POOL_GW = W_POOL // len(POOL_WINDOWS)
POOL_HIST = max(POOL_WINDOWS) - 1
W_SSD = D_INNER - W_POOL
HEAD_DIM = 64
HEADS = W_SSD // HEAD_DIM
GROUPS = 8
HPG = HEADS // GROUPS
GROUP_W = HPG * HEAD_DIM
STATE = 128
CONV_W = 4
CONV_DIM = W_SSD + 2 * GROUPS * STATE
MAIN_DIM = 2 * W_POOL + W_SSD + CONV_DIM
ALPHA = (2 * DEPTH) ** 0.25
LN_EPS = 1e-5
RMS_EPS = 1e-5
LOG2E = math.log2(math.e)

LANES = 128
POOL_BUF = 16
CONV_BUF = 8
MIX_T = 128
PAIRS_PER_GROUP = GROUP_W // LANES
assert MIX_T == STATE == LANES and 2 * HEAD_DIM == LANES
VMEM_LIMIT = 56 * 1024 * 1024

OFF_U, OFF_G, OFF_Z, OFF_XBC = 0, W_POOL, 2 * W_POOL, 2 * W_POOL + W_SSD


def _silu_of_double(h):
    return h * jnp.tanh(h) + h


def _silu(x):
    return _silu_of_double(0.5 * x)


def _layer_norm(x):
    mu = jnp.mean(x, axis=-1, keepdims=True)
    xc = x - mu
    var = jnp.mean(xc * xc, axis=-1, keepdims=True)
    return xc * lax.rsqrt(var + LN_EPS)


def _ada_kernel(c_ref, w_ref, b_ref, o_ref):
    sc = _silu(c_ref[...]).astype(BF16)
    o_ref[...] = jnp.dot(sc, w_ref[...].astype(BF16), preferred_element_type=F32) + b_ref[...]


def _ada(c_all, w_ada, b_ada, tn=512):
    rows = c_all.shape[0]
    n = w_ada.shape[-1]
    return pl.pallas_call(
        _ada_kernel,
        grid=(DEPTH, n // tn),
        in_specs=[
            pl.BlockSpec((rows, D_MODEL), lambda l, j: (0, 0)),
            pl.BlockSpec((None, D_MODEL, tn), lambda l, j: (l, 0, j)),
            pl.BlockSpec((None, 1, tn), lambda l, j: (l, 0, j)),
        ],
        out_specs=pl.BlockSpec((None, rows, tn), lambda l, j: (l, 0, j)),
        out_shape=jax.ShapeDtypeStruct((DEPTH, rows, n), F32),
        compiler_params=pltpu.CompilerParams(
            dimension_semantics=("arbitrary", "arbitrary"), vmem_limit_bytes=VMEM_LIMIT),
        name="ada_mod",
    )(c_all, w_ada, b_ada.reshape(DEPTH, 1, n))


def _lnmod_kernel(x_ref, sc_ref, sh_ref, wdt_ref, h_ref, dt_ref, *, sub):
    n_mod = sc_ref.shape[0]
    rows_per_mod = x_ref.shape[0] // n_mod
    step = min(sub, rows_per_mod)
    for r in range(n_mod):
        for s0 in range(0, rows_per_mod, step):
            rs = slice(r * rows_per_mod + s0, r * rows_per_mod + s0 + step)
            h = _layer_norm(x_ref[rs, :]) * (1.0 + sc_ref[r:r + 1, :]) + sh_ref[r:r + 1, :]
            h_ref[rs, :] = h.astype(BF16)
    dt_ref[...] = jnp.dot(h_ref[...], wdt_ref[...], preferred_element_type=F32)


def _lnmod(x2d, scale3, shift3, w_dt_b, layer, tm):
    m = x2d.shape[0]
    n_mod = scale3.shape[1]
    tiles_per_mod_block = (m // tm) // scale3.shape[0]
    mod_spec = pl.BlockSpec((None, n_mod, D_MODEL), lambda i: (i // tiles_per_mod_block, 0, 0))
    return pl.pallas_call(
        functools.partial(_lnmod_kernel, sub=128),
        grid=(m // tm,),
        in_specs=[
            pl.BlockSpec((tm, D_MODEL), lambda i: (i, 0)),
            mod_spec,
            mod_spec,
            pl.BlockSpec((None, D_MODEL, LANES), lambda i: (layer, 0, 0)),
        ],
        out_specs=[
            pl.BlockSpec((tm, D_MODEL), lambda i: (i, 0)),
            pl.BlockSpec((tm, LANES), lambda i: (i, 0)),
        ],
        out_shape=[
            jax.ShapeDtypeStruct((m, D_MODEL), BF16),
            jax.ShapeDtypeStruct((m, LANES), F32),
        ],
        compiler_params=pltpu.CompilerParams(
            dimension_semantics=("parallel",), vmem_limit_bytes=VMEM_LIMIT),
        name="ln_mod",
    )(x2d, scale3, shift3, w_dt_b)


def _inproj_kernel(*refs, mode, n_seg, has_init, tiles_per_seq, sub, side_cast):
    refs = list(refs)
    if side_cast:
        n_in = {"conv": 5 if has_init else 4}.get(mode, 2)
        side_ref = refs.pop(n_in)
        side_b_ref = refs.pop(n_in + (2 if mode == "conv" else 1))
        side_b_ref[...] = side_ref[...].astype(BF16)
    if mode == "conv":
        if has_init:
            h_ref, w_ref, cw_ref, cb_ref, hist_ref, o_ref, st_ref, wb_ref, buf = refs
        else:
            h_ref, w_ref, cw_ref, cb_ref, o_ref, st_ref, wb_ref, buf = refs
    else:
        h_ref, w_ref, o_ref, wb_ref = refs
    i = pl.program_id(1)

    @pl.when(i == 0)
    def _():
        wb_ref[...] = w_ref[...].astype(BF16)

    def project(r0, n):
        return lax.dot_general(h_ref[r0:r0 + n, :], wb_ref[...], (((1,), (1,)), ((), ())),
                               preferred_element_type=F32)

    tm = h_ref.shape[0]
    if mode == "raw":
        o_ref[...] = project(0, tm)
    elif mode == "silu":
        o_ref[...] = _silu(project(0, tm))
    else:
        seg = tm // n_seg
        half = tm // 2 if (n_seg == 1 and tm >= 2 * sub) else tm
        if not has_init:
            @pl.when(i % tiles_per_seq == 0)
            def _():
                buf[0, 0:CONV_BUF, :] = jnp.zeros((CONV_BUF, buf.shape[-1]), F32)
        for s in range(n_seg):
            if has_init:
                buf[s, 0:CONV_BUF, :] = hist_ref[s]
        cw_half = 0.5 * cw_ref[...]
        cb_half = 0.5 * cb_ref[...]
        for h0 in range(0, tm, half):
            res = project(h0, half)
            for s in range(h0 // seg, max(h0 // seg + 1, (h0 + half) // seg)):
                lo = max(h0, s * seg) - s * seg
                hi = min(h0 + half, (s + 1) * seg) - s * seg
                buf[s, CONV_BUF + lo:CONV_BUF + hi, :] = res[s * seg + lo - h0:s * seg + hi - h0, :]
                for r0 in range(lo, hi, sub):
                    n = min(sub, hi - r0)
                    acc = cb_half + buf[s, CONV_BUF + r0:CONV_BUF + r0 + n, :] * cw_half[CONV_W - 1:CONV_W, :]
                    for back in range(1, CONV_W):
                        acc = acc + (buf[s, CONV_BUF + r0 - back:CONV_BUF + r0 - back + n, :]
                                     * cw_half[CONV_W - 1 - back:CONV_W - back, :])
                    o_ref[s * seg + r0:s * seg + r0 + n, :] = _silu_of_double(acc)
        for s in range(n_seg):
            st_ref[s] = buf[s, seg:seg + CONV_BUF, :]
            if not has_init:
                buf[s, 0:CONV_BUF, :] = buf[s, seg:seg + CONV_BUF, :]


def _inproj(h2d, w_in_t, layer, col0, width, mode, tm, tn, conv=None, n_seg=1, tiles_per_seq=1, side=None):
    m = h2d.shape[0]
    j0 = col0 // tn
    m_tiles = m // tm
    in_specs = [
        pl.BlockSpec((tm, D_MODEL), lambda j, i: (i, 0)),
        pl.BlockSpec((None, tn, D_MODEL), lambda j, i: (layer, j0 + j, 0)),
    ]
    args = [h2d, w_in_t]
    out_specs = pl.BlockSpec((tm, tn), lambda j, i: (i, j))
    out_shape = jax.ShapeDtypeStruct((m, width), F32)
    scratch = [pltpu.VMEM((tn, D_MODEL), BF16)]
    has_init = False
    if mode == "conv":
        conv_w, conv_b, hist = conv
        has_init = hist is not None
        in_specs += [pl.BlockSpec((None, CONV_W, tn), lambda j, i: (layer, 0, j)),
                     pl.BlockSpec((None, 1, tn), lambda j, i: (layer, 0, j))]
        args += [conv_w, conv_b]
        if has_init:
            in_specs.append(pl.BlockSpec((None, n_seg, CONV_BUF, tn), lambda j, i: (layer, 0, 0, j)))
            args.append(hist)
        n_seq = (m // tm) * n_seg // tiles_per_seq
        out_specs = [out_specs,
                     pl.BlockSpec((n_seg, CONV_BUF, tn), lambda j, i: (i // tiles_per_seq, 0, j))]
        out_shape = [out_shape, jax.ShapeDtypeStruct((n_seq, CONV_BUF, width), F32)]
        scratch.append(pltpu.VMEM((n_seg, CONV_BUF + tm // n_seg, tn), F32))
    if side is not None:
        _, side_rows, side_cols = side.shape
        slab = side_rows // ((width // tn) * m_tiles)
        in_specs.append(pl.BlockSpec((None, slab, side_cols), lambda j, i: (layer, j * m_tiles + i, 0)))
        args.append(side)
        out_specs = (out_specs if isinstance(out_specs, list) else [out_specs]) + [
            pl.BlockSpec((slab, side_cols), lambda j, i: (j * m_tiles + i, 0))]
        out_shape = (out_shape if isinstance(out_shape, list) else [out_shape]) + [
            jax.ShapeDtypeStruct((side_rows, side_cols), BF16)]
    return pl.pallas_call(
        functools.partial(_inproj_kernel, mode=mode, n_seg=n_seg, has_init=has_init,
                          tiles_per_seq=tiles_per_seq, sub=256, side_cast=side is not None),
        grid=(width // tn, m // tm),
        in_specs=in_specs,
        out_specs=out_specs,
        out_shape=out_shape,
        scratch_shapes=scratch,
        compiler_params=pltpu.CompilerParams(
            dimension_semantics=("arbitrary", "arbitrary"), vmem_limit_bytes=VMEM_LIMIT),
        name="in_proj_" + mode,
    )(*args)


def _outproj_kernel(m_ref, w_ref, x_ref, g_ref, lng_ref, lnb_ref, o_ref, acc_ref, *, nj, tn, sub):
    j = pl.program_id(1)
    n_mod = g_ref.shape[0]
    tm = m_ref.shape[0]
    rows_per_mod = tm // n_mod
    o = jnp.dot(m_ref[...], w_ref[...], preferred_element_type=F32)
    for r in range(n_mod):
        rs = slice(r * rows_per_mod, (r + 1) * rows_per_mod)
        acc_ref[j, rs, :] = ALPHA * x_ref[rs, :] + g_ref[r:r + 1, :] * o[rs, :]

    @pl.when(j == nj - 1)
    def _():
        inv_d = 1.0 / (nj * tn)
        for s0 in range(0, tm, sub):
            rs = slice(s0, s0 + sub)
            total = acc_ref[0, rs, :].sum(axis=-1, keepdims=True)
            for jj in range(1, nj):
                total = total + acc_ref[jj, rs, :].sum(axis=-1, keepdims=True)
            mu = total * inv_d
            sq = jnp.zeros_like(mu)
            for jj in range(nj):
                d = acc_ref[jj, rs, :] - mu
                sq = sq + (d * d).sum(axis=-1, keepdims=True)
            inv = lax.rsqrt(sq * inv_d + LN_EPS)
            for jj in range(nj):
                cs = slice(jj * tn, (jj + 1) * tn)
                o_ref[rs, cs] = (acc_ref[jj, rs, :] - mu) * inv * lng_ref[:, cs] + lnb_ref[:, cs]


def _outproj(mixed, w_out_b, x2d, gate3, ln_g, ln_b, layer, tm, tn):
    m = x2d.shape[0]
    n_mod = gate3.shape[1]
    tiles_per_mod_block = (m // tm) // gate3.shape[0]
    nj = D_MODEL // tn
    vec_spec = pl.BlockSpec((None, 1, D_MODEL), lambda i, j: (layer, 0, 0))
    return pl.pallas_call(
        functools.partial(_outproj_kernel, nj=nj, tn=tn, sub=min(128, tm)),
        grid=(m // tm, nj),
        in_specs=[
            pl.BlockSpec((tm, D_INNER), lambda i, j: (i, 0)),
            pl.BlockSpec((D_INNER, tn), lambda i, j: (0, j)),
            pl.BlockSpec((tm, tn), lambda i, j: (i, j)),
            pl.BlockSpec((None, n_mod, tn), lambda i, j: (i // tiles_per_mod_block, 0, j)),
            vec_spec,
            vec_spec,
        ],
        out_specs=pl.BlockSpec((tm, D_MODEL), lambda i, j: (i, 0)),
        out_shape=jax.ShapeDtypeStruct((m, D_MODEL), F32),
        scratch_shapes=[pltpu.VMEM((nj, tm, tn), F32)],
        compiler_params=pltpu.CompilerParams(
            dimension_semantics=("parallel", "arbitrary"), vmem_limit_bytes=VMEM_LIMIT),
        name="out_proj",
    )(mixed, w_out_b, x2d, gate3, ln_g.reshape(DEPTH, 1, D_MODEL), ln_b.reshape(DEPTH, 1, D_MODEL))


def _split3(a):
    hi = a.astype(BF16)
    r1 = a - hi.astype(F32)
    mid = r1.astype(BF16)
    lo = (r1 - mid.astype(F32)).astype(BF16)
    return hi, mid, lo


def _mixer_kernel(*refs, tv, nchunks, start_pos, has_init, has_stack, layer):
    if has_init:
        (u_ref, gz_ref, xc_ref, dtr_ref, pool0_ref, ssm0_ref, *rest) = refs
    else:
        (u_ref, gz_ref, xc_ref, dtr_ref, *rest) = refs
    (wpool_ref, pscale_ref, dtb_ref, alog_ref, dskip_ref, normw_ref, *rest) = rest
    if has_stack:
        rest = rest[1:]
    mixed_ref, npool_ref, nssm_ref, poolbuf, winbuf, h_ref = rest
    t = MIX_T
    c = pl.program_id(1)

    def pad_rows(v):
        if tv == t:
            return v
        return jnp.concatenate([v, jnp.zeros((t - tv, v.shape[1]), v.dtype)], axis=0)

    @pl.when(c == 0)
    def _init():
        if has_init:
            poolbuf[0:POOL_BUF, :] = pool0_ref[...]
            for g in range(GROUPS):
                for j in range(PAIRS_PER_GROUP):
                    r0 = (g * HPG + 2 * j) * HEAD_DIM
                    h_ref[g, :, j * LANES:(j + 1) * LANES] = ssm0_ref[r0:r0 + LANES, :].T
        else:
            poolbuf[0:POOL_BUF, :] = jnp.zeros((POOL_BUF, W_POOL), F32)
            h_ref[...] = jnp.zeros(h_ref.shape, F32)

    poolbuf[POOL_BUF:POOL_BUF + tv, :] = u_ref[...]
    row = lax.broadcasted_iota(jnp.int32, (tv, 1), 0)
    pos = start_pos + c * tv + row
    for g, w in enumerate(POOL_WINDOWS):
        cols = slice(g * POOL_GW, (g + 1) * POOL_GW)
        u_g = poolbuf[POOL_BUF:POOL_BUF + tv, cols]
        src, d, level = poolbuf, 1, 0
        while 2 * d < w:
            lo = POOL_BUF - (w - 2 * d)
            dst = winbuf.at[level % 2]
            dst[lo:POOL_BUF + tv, cols] = src[lo:POOL_BUF + tv, cols] + src[lo - d:POOL_BUF + tv - d, cols]
            src, d, level = dst, 2 * d, level + 1
        s = src[POOL_BUF:POOL_BUF + tv, cols] + src[POOL_BUF - d:POOL_BUF + tv - d, cols]
        cnt = jnp.minimum(pos + 1, w).astype(F32)
        pooled = (s / cnt - u_g).astype(BF16)
        po = jnp.dot(pooled, wpool_ref[g], preferred_element_type=F32)
        mixed_ref[:, cols] = (po * pscale_ref[:, cols] * gz_ref[:, cols]).astype(BF16)
    poolbuf[0:POOL_BUF, :] = poolbuf[tv:tv + POOL_BUF, :]

    x_dt = dtr_ref[...] + dtb_ref[...]
    dt = jnp.maximum(x_dt, 0.0) + jnp.log1p(jnp.exp(-jnp.abs(x_dt)))
    if tv < t:
        dt = jnp.concatenate([dt, jnp.zeros((t - tv, LANES), F32)], axis=0)
    a = dt * (-LOG2E * jnp.exp(alog_ref[...]))
    ri = lax.broadcasted_iota(jnp.int32, (t, t), 0)
    ci = lax.broadcasted_iota(jnp.int32, (t, t), 1)
    tril = ri >= ci
    tri = jnp.where(tril, 1.0, 0.0).astype(BF16)
    a_cs = sum(jnp.dot(tri, piece, preferred_element_type=F32) for piece in _split3(a))
    a_last = a_cs[t - 1:t, :]
    ea_last = jnp.exp2(a_last)
    a_cs_t = a_cs.T
    dt_t = dt.T
    w_t = (jnp.exp2(a_last - a_cs) * dt).T
    lo_half = lax.broadcasted_iota(jnp.int32, (1, LANES), 1) < HEAD_DIM

    def block_diag(v):
        zero = jnp.zeros_like(v)
        return jnp.concatenate([jnp.where(lo_half, v, zero), jnp.where(lo_half, zero, v)], axis=0)

    for g in range(GROUPS):
        b_g = pad_rows(xc_ref[:, W_SSD + g * STATE:W_SSD + (g + 1) * STATE])
        c_g = pad_rows(xc_ref[:, W_SSD + GROUPS * STATE + g * STATE:W_SSD + GROUPS * STATE + (g + 1) * STATE])
        cb = lax.dot_general(c_g.astype(BF16), b_g.astype(BF16), (((1,), (1,)), ((), ())),
                             preferred_element_type=F32)
        b_t = b_g.T
        ys = []
        for j in range(PAIRS_PER_GROUP):
            p = g * PAIRS_PER_GROUP + j
            pc = slice(p * LANES, (p + 1) * LANES)
            jc = slice(j * LANES, (j + 1) * LANES)
            xs_p = pad_rows(xc_ref[:, pc])
            h_p = h_ref[g, :, jc]
            bd_x = block_diag(xs_p.astype(BF16))
            bd_h = block_diag(h_p.astype(BF16))
            m_parts, e_parts, b_parts = [], [], []
            for k in (2 * p, 2 * p + 1):
                col = jnp.broadcast_to(a_cs[0:tv, k:k + 1], (tv, t))
                lmat = jnp.where(tril[0:tv], jnp.exp2(col - a_cs_t[k:k + 1, :]), 0.0)
                m_parts.append((cb[0:tv] * lmat * dt_t[k:k + 1, :]).astype(BF16))
                e_parts.append((c_g[0:tv] * jnp.exp2(col)).astype(BF16))
                b_parts.append((b_t * w_t[k:k + 1, :]).astype(BF16))
            y_mm = jnp.dot(jnp.concatenate(m_parts + e_parts, axis=1),
                           jnp.concatenate([bd_x, bd_h], axis=0), preferred_element_type=F32)
            ys.append(y_mm + dskip_ref[:, pc] * xs_p[0:tv])
            st = jnp.dot(jnp.concatenate(b_parts, axis=1), bd_x, preferred_element_type=F32)
            cd = jnp.where(lo_half, ea_last[:, 2 * p:2 * p + 1], ea_last[:, 2 * p + 1:2 * p + 2])
            h_ref[g, :, jc] = cd * h_p + st
        gc = slice(g * GROUP_W, (g + 1) * GROUP_W)
        y_g = jnp.concatenate(ys, axis=1)
        y_g = y_g * gz_ref[:, W_POOL + g * GROUP_W:W_POOL + (g + 1) * GROUP_W]
        ms = jnp.mean(y_g * y_g, axis=-1, keepdims=True)
        out = y_g * lax.rsqrt(ms + RMS_EPS) * normw_ref[:, gc]
        mixed_ref[:, W_POOL + g * GROUP_W:W_POOL + (g + 1) * GROUP_W] = out.astype(BF16)

    @pl.when(c == nchunks - 1)
    def _fin():
        npool_ref[...] = poolbuf[0:POOL_BUF, :]
        if has_stack:
            state_out = nssm_ref
        else:
            state_out = nssm_ref.at[layer]
            for other in range(DEPTH):
                if other != layer:
                    nssm_ref[other] = jnp.zeros((W_SSD, STATE), F32)
        for g in range(GROUPS):
            for j in range(PAIRS_PER_GROUP):
                r0 = (g * HPG + 2 * j) * HEAD_DIM
                state_out[r0:r0 + LANES, :] = h_ref[g, :, j * LANES:(j + 1) * LANES].T


def _mixer(u, gz, xc, dt_raw, init, ssm_stack, params, layer, nb, seq, start_pos):
    tv = min(seq, MIX_T)
    nchunks = seq // tv
    has_init = init is not None
    has_stack = ssm_stack is not None
    row_map = lambda b, c: (b * nchunks + c, 0)
    per_layer2 = lambda b, c: (layer, 0, 0)
    per_b = lambda b, c: (b, 0, 0)
    per_layer_b = lambda b, c: (layer, b, 0, 0)
    in_specs = [pl.BlockSpec((tv, W_POOL), row_map), pl.BlockSpec((tv, W_POOL + W_SSD), row_map),
                pl.BlockSpec((tv, CONV_DIM), row_map), pl.BlockSpec((tv, LANES), row_map)]
    args = [u, gz, xc, dt_raw]
    if has_init:
        in_specs += [pl.BlockSpec((None, None, POOL_BUF, W_POOL), per_layer_b),
                     pl.BlockSpec((None, None, W_SSD, STATE), per_layer_b)]
        args += list(init)
    in_specs += [
        pl.BlockSpec((None, len(POOL_WINDOWS), POOL_GW, POOL_GW), lambda b, c: (layer, 0, 0, 0)),
        pl.BlockSpec((None, 1, W_POOL), per_layer2),
        pl.BlockSpec((None, 1, LANES), per_layer2),
        pl.BlockSpec((None, 1, LANES), per_layer2),
        pl.BlockSpec((None, 1, W_SSD), per_layer2),
        pl.BlockSpec((None, 1, W_SSD), per_layer2),
    ]
    args += list(params)
    aliases = {}
    if has_stack:
        in_specs.append(pl.BlockSpec(memory_space=pl.ANY))
        args.append(ssm_stack)
        aliases = {len(args) - 1: 2}
    return pl.pallas_call(
        functools.partial(_mixer_kernel, tv=tv, nchunks=nchunks, start_pos=start_pos, has_init=has_init,
                          has_stack=has_stack, layer=layer),
        grid=(nb, nchunks),
        in_specs=in_specs,
        out_specs=[
            pl.BlockSpec((tv, D_INNER), row_map),
            pl.BlockSpec((None, POOL_BUF, W_POOL), per_b),
            (pl.BlockSpec((None, None, W_SSD, STATE), per_layer_b) if has_stack
             else pl.BlockSpec((DEPTH, None, W_SSD, STATE), lambda b, c: (0, b, 0, 0))),
        ],
        out_shape=[
            jax.ShapeDtypeStruct((nb * seq, D_INNER), BF16),
            jax.ShapeDtypeStruct((nb, POOL_BUF, W_POOL), F32),
            jax.ShapeDtypeStruct((DEPTH, nb, W_SSD, STATE), F32),
        ],
        input_output_aliases=aliases,
        scratch_shapes=[
            pltpu.VMEM((POOL_BUF + tv, W_POOL), F32),
            pltpu.VMEM((2, POOL_BUF + tv, W_POOL), F32),
            pltpu.VMEM((GROUPS, STATE, GROUP_W), F32),
        ],
        compiler_params=pltpu.CompilerParams(
            dimension_semantics=("parallel", "arbitrary"), vmem_limit_bytes=VMEM_LIMIT),
        name="mixer_init" if has_init else "mixer_zero",
    )(*args)


def _pad_rows_front(a, rows):
    pad = rows - a.shape[-2]
    return jnp.pad(a, [(0, 0)] * (a.ndim - 2) + [(pad, 0), (0, 0)])


def _pad_lanes(a):
    return jnp.pad(a, [(0, 0)] * (a.ndim - 1) + [(0, LANES - a.shape[-1])])


def kernel(x_prompt, x_sample, state_pool, state_conv, state_ssm, c_prompt, c_sample,
           w_ada, b_ada, w_in, w_pool, pool_scale, conv_w, conv_b,
           dt_bias, a_log, d_skip, ssd_norm_w, w_out, ln_g, ln_b):
    nbp, seq_p, _ = x_prompt.shape
    nbs, seq_s, _ = x_sample.shape
    past_len = 1024

    n_c = nbp + nbs
    c_rows = -(-n_c // 8) * 8
    c_all = jnp.concatenate([c_prompt, c_sample, jnp.zeros((c_rows - n_c, D_MODEL), F32)], axis=0)
    mod = _ada(c_all, w_ada, b_ada)

    w_in_t = jnp.swapaxes(w_in, 1, 2)
    w_dt_b = _pad_lanes(w_in[:, :, MAIN_DIM:]).astype(BF16)
    conv_b3 = conv_b[:, None]
    params = (w_pool.astype(BF16), pool_scale[:, None],
              _pad_lanes(dt_bias)[:, None], _pad_lanes(a_log)[:, None],
              jnp.repeat(d_skip, HEAD_DIM, axis=-1)[:, None], ssd_norm_w[:, None])
    conv0 = _pad_rows_front(state_conv, CONV_BUF)
    init = (_pad_rows_front(state_pool, POOL_BUF),
            state_ssm.reshape(DEPTH, nbs, W_SSD, STATE))

    def in_projections(h, l, tm, hist, n_seg, tiles_per_seq, side=None):
        u = _inproj(h, w_in_t, l, OFF_U, W_POOL, "raw", 512, 1024)
        gz = _inproj(h, w_in_t, l, OFF_G, W_POOL + W_SSD, "silu", 512, 1024)
        xc, nconv, *side_b = _inproj(h, w_in_t, l, OFF_XBC, CONV_DIM, "conv", tm, 512, side=side,
                                     conv=(conv_w, conv_b3, hist), n_seg=n_seg, tiles_per_seq=tiles_per_seq)
        return (u, gz, xc, nconv, *side_b)

    xp = x_prompt.reshape(nbp * seq_p, D_MODEL)
    xs = x_sample.reshape(nbs * seq_s, D_MODEL)
    outs = {k: [] for k in ("pp", "pc", "psm", "sp", "sc", "ssm")}
    tm_p = 1024
    ssm_p = ssm_s = None
    for l in range(DEPTH):
        shift, scale, gate = (mod[l, :, i * D_MODEL:(i + 1) * D_MODEL] for i in range(3))

        sc3, sh3, g3 = (v[0:nbp].reshape(nbp, 1, D_MODEL) for v in (scale, shift, gate))
        h, dt_raw = _lnmod(xp, sc3, sh3, w_dt_b, l, tm=512)
        u, gz, xc, nconv, w_out_b = in_projections(h, l, tm_p, None, 1, seq_p // tm_p, side=w_out)
        mixed, npool, ssm_p = _mixer(u, gz, xc, dt_raw, None, ssm_p, params, l, nbp, seq_p, 0)
        xp = _outproj(mixed, w_out_b, xp, g3, ln_g, ln_b, l, tm=512, tn=256)
        outs["pp"].append(npool[:, 1:])
        outs["pc"].append(nconv[:, CONV_BUF - (CONV_W - 1):])

        sc3, sh3, g3 = (v[nbp:n_c].reshape(1, nbs, D_MODEL) for v in (scale, shift, gate))
        h, dt_raw = _lnmod(xs, sc3, sh3, w_dt_b, l, tm=nbs * seq_s)
        u, gz, xc, nconv = in_projections(h, l, nbs * seq_s, conv0, nbs, 1)
        mixed, npool, ssm_s = _mixer(u, gz, xc, dt_raw, init, ssm_s, params, l, nbs, seq_s, past_len)
        xs = _outproj(mixed, w_out_b, xs, g3, ln_g, ln_b, l, tm=nbs * seq_s, tn=256)
        outs["sp"].append(npool[:, 1:])
        outs["sc"].append(nconv[:, CONV_BUF - (CONV_W - 1):])

    return (xp.reshape(nbp, seq_p, D_MODEL), xs.reshape(nbs, seq_s, D_MODEL),
            jnp.stack(outs["pp"]), jnp.stack(outs["pc"]), ssm_p.reshape(DEPTH, nbp, HEADS, HEAD_DIM, STATE),
            jnp.stack(outs["sp"]), jnp.stack(outs["sc"]), ssm_s.reshape(DEPTH, nbs, HEADS, HEAD_DIM, STATE))
```

```python
import functools
import math

import jax
import jax.numpy as jnp
from jax import lax
from jax.experimental import pallas as pl
from jax.experimental.pallas import tpu as pltpu

F32 = jnp.float32
BF16 = jnp.bfloat16

D_MODEL = 4096
DEPTH = 2
D_INNER = 2 * D_MODEL
W_POOL = D_INNER // 4
POOL_WINDOWS = (2, 4, 8, 16)
POOL_GW = W_POOL // len(POOL_WINDOWS)
POOL_HIST = max(POOL_WINDOWS) - 1
W_SSD = D_INNER - W_POOL
HEAD_DIM = 64
HEADS = W_SSD // HEAD_DIM
GROUPS = 8
HPG = HEADS // GROUPS
GROUP_W = HPG * HEAD_DIM
STATE = 128
CONV_W = 4
CONV_DIM = W_SSD + 2 * GROUPS * STATE
MAIN_DIM = 2 * W_POOL + W_SSD + CONV_DIM
ALPHA = (2 * DEPTH) ** 0.25
LN_EPS = 1e-5
RMS_EPS = 1e-5
LOG2E = math.log2(math.e)

LANES = 128
POOL_BUF = 16
CONV_BUF = 8
MIX_T = 128
PAIRS_PER_GROUP = GROUP_W // LANES
assert MIX_T == STATE == LANES and 2 * HEAD_DIM == LANES
VMEM_LIMIT = 56 * 1024 * 1024
RING_SLOTS = 3
OUT_TN = 256

OFF_U, OFF_G, OFF_Z, OFF_XBC = 0, W_POOL, 2 * W_POOL, 2 * W_POOL + W_SSD


def _silu_of_double(h):
    return h * jnp.tanh(h) + h


def _silu(x):
    return _silu_of_double(0.5 * x)


def _layer_norm(x):
    mu = jnp.mean(x, axis=-1, keepdims=True)
    xc = x - mu
    var = jnp.mean(xc * xc, axis=-1, keepdims=True)
    return xc * lax.rsqrt(var + LN_EPS)


def _ada_kernel(c_ref, w_ref, b_ref, o_ref):
    sc = _silu(c_ref[...]).astype(BF16)
    o_ref[...] = jnp.dot(sc, w_ref[...].astype(BF16), preferred_element_type=F32) + b_ref[...]


def _ada(c_all, w_ada, b_ada, tn=512):
    rows = c_all.shape[0]
    n = w_ada.shape[-1]
    return pl.pallas_call(
        _ada_kernel,
        grid=(DEPTH, n // tn),
        in_specs=[
            pl.BlockSpec((rows, D_MODEL), lambda l, j: (0, 0)),
            pl.BlockSpec((None, D_MODEL, tn), lambda l, j: (l, 0, j)),
            pl.BlockSpec((None, 1, tn), lambda l, j: (l, 0, j)),
        ],
        out_specs=pl.BlockSpec((None, rows, tn), lambda l, j: (l, 0, j)),
        out_shape=jax.ShapeDtypeStruct((DEPTH, rows, n), F32),
        compiler_params=pltpu.CompilerParams(
            dimension_semantics=("arbitrary", "arbitrary"), vmem_limit_bytes=VMEM_LIMIT),
        name="ada_mod",
    )(c_all, w_ada, b_ada.reshape(DEPTH, 1, n))


def _lnmod_kernel(x_ref, sc_ref, sh_ref, wdt_ref, h_ref, dt_ref, *, sub):
    n_mod = sc_ref.shape[0]
    rows_per_mod = x_ref.shape[0] // n_mod
    step = min(sub, rows_per_mod)
    for r in range(n_mod):
        for s0 in range(0, rows_per_mod, step):
            rs = slice(r * rows_per_mod + s0, r * rows_per_mod + s0 + step)
            h = _layer_norm(x_ref[rs, :]) * (1.0 + sc_ref[r:r + 1, :]) + sh_ref[r:r + 1, :]
            h_ref[rs, :] = h.astype(BF16)
    dt_ref[...] = jnp.dot(h_ref[...], wdt_ref[...], preferred_element_type=F32)


def _lnmod(x2d, scale3, shift3, w_dt_b, layer, tm):
    m = x2d.shape[0]
    n_mod = scale3.shape[1]
    tiles_per_mod_block = (m // tm) // scale3.shape[0]
    mod_spec = pl.BlockSpec((None, n_mod, D_MODEL), lambda i: (i // tiles_per_mod_block, 0, 0))
    return pl.pallas_call(
        functools.partial(_lnmod_kernel, sub=128),
        grid=(m // tm,),
        in_specs=[
            pl.BlockSpec((tm, D_MODEL), lambda i: (i, 0)),
            mod_spec,
            mod_spec,
            pl.BlockSpec((None, D_MODEL, LANES), lambda i: (layer, 0, 0)),
        ],
        out_specs=[
            pl.BlockSpec((tm, D_MODEL), lambda i: (i, 0)),
            pl.BlockSpec((tm, LANES), lambda i: (i, 0)),
        ],
        out_shape=[
            jax.ShapeDtypeStruct((m, D_MODEL), BF16),
            jax.ShapeDtypeStruct((m, LANES), F32),
        ],
        compiler_params=pltpu.CompilerParams(
            dimension_semantics=("parallel",), vmem_limit_bytes=VMEM_LIMIT),
        name="ln_mod",
    )(x2d, scale3, shift3, w_dt_b)


def _ring_fetch(src_tile, bufs, sems, step, n_steps):
    slots = bufs.shape[0]

    def copy(k):
        return pltpu.make_async_copy(src_tile(k), bufs.at[k % slots], sems.at[k % slots])

    @pl.when(step == 0)
    def _():
        for k in range(min(slots - 1, n_steps)):
            copy(k).start()

    @pl.when(step + (slots - 1) < n_steps)
    def _():
        copy(step + (slots - 1)).start()

    copy(step).wait()
    return bufs.at[step % slots]


def _inproj_kernel(*refs, mode, n_seg, has_init, tiles_per_seq, sub, side_cast, ring, tm, m_tiles, n_steps):
    refs = list(refs)
    if ring:
        hsem = refs.pop()
        hbuf = refs.pop()
    if side_cast:
        n_in = {"conv": 5 if has_init else 4}.get(mode, 2)
        side_ref = refs.pop(n_in)
        side_b_ref = refs.pop(n_in + (2 if mode == "conv" else 1))
        for jj in range(side_b_ref.shape[0]):
            cw = side_b_ref.shape[2]
            side_b_ref[jj] = side_ref[:, jj * cw:(jj + 1) * cw].astype(BF16)
    if mode == "conv":
        if has_init:
            h_ref, w_ref, cw_ref, cb_ref, hist_ref, o_ref, st_ref, wb_ref, buf = refs
        else:
            h_ref, w_ref, cw_ref, cb_ref, o_ref, st_ref, wb_ref, buf = refs
    else:
        h_ref, w_ref, o_ref, wb_ref = refs
    i = pl.program_id(1)
    if ring:
        h_hbm = h_ref

        def h_rows(k):
            row0 = (k % m_tiles) * tm
            return h_hbm.at[pl.ds(row0 if isinstance(row0, int) else pl.multiple_of(row0, tm), tm)]

        h_ref = _ring_fetch(h_rows, hbuf, hsem, pl.program_id(0) * m_tiles + i, n_steps)

    @pl.when(i == 0)
    def _():
        wb_ref[...] = w_ref[...].astype(BF16)

    def project(r0, n):
        return lax.dot_general(h_ref[r0:r0 + n, :], wb_ref[...], (((1,), (1,)), ((), ())),
                               preferred_element_type=F32)

    if mode == "raw":
        o_ref[...] = project(0, tm)
    elif mode == "silu":
        o_ref[...] = _silu(project(0, tm))
    else:
        seg = tm // n_seg
        half = tm // 2 if (n_seg == 1 and tm >= 2 * sub) else tm
        if not has_init:
            @pl.when(i % tiles_per_seq == 0)
            def _():
                buf[0, 0:CONV_BUF, :] = jnp.zeros((CONV_BUF, buf.shape[-1]), F32)
        for s in range(n_seg):
            if has_init:
                buf[s, 0:CONV_BUF, :] = hist_ref[s]
        cw_half = 0.5 * cw_ref[...]
        cb_half = 0.5 * cb_ref[...]
        for h0 in range(0, tm, half):
            res = project(h0, half)
            for s in range(h0 // seg, max(h0 // seg + 1, (h0 + half) // seg)):
                lo = max(h0, s * seg) - s * seg
                hi = min(h0 + half, (s + 1) * seg) - s * seg
                buf[s, CONV_BUF + lo:CONV_BUF + hi, :] = res[s * seg + lo - h0:s * seg + hi - h0, :]
                for r0 in range(lo, hi, sub):
                    n = min(sub, hi - r0)
                    acc = cb_half + buf[s, CONV_BUF + r0:CONV_BUF + r0 + n, :] * cw_half[CONV_W - 1:CONV_W, :]
                    for back in range(1, CONV_W):
                        acc = acc + (buf[s, CONV_BUF + r0 - back:CONV_BUF + r0 - back + n, :]
                                     * cw_half[CONV_W - 1 - back:CONV_W - back, :])
                    o_ref[s * seg + r0:s * seg + r0 + n, :] = _silu_of_double(acc)
        for s in range(n_seg):
            st_ref[s] = buf[s, seg:seg + CONV_BUF, :]
            if not has_init:
                buf[s, 0:CONV_BUF, :] = buf[s, seg:seg + CONV_BUF, :]


def _inproj(h2d, w_in_t, layer, col0, width, mode, tm, tn, conv=None, n_seg=1, tiles_per_seq=1, side=None,
            side_tn=None, ring=False):
    m = h2d.shape[0]
    j0 = col0 // tn
    m_tiles = m // tm
    in_specs = [
        pl.BlockSpec(memory_space=pl.ANY) if ring else pl.BlockSpec((tm, D_MODEL), lambda j, i: (i, 0)),
        pl.BlockSpec((None, tn, D_MODEL), lambda j, i: (layer, j0 + j, 0)),
    ]
    args = [h2d, w_in_t]
    out_specs = pl.BlockSpec((tm, tn), lambda j, i: (i, j))
    out_shape = jax.ShapeDtypeStruct((m, width), F32)
    scratch = [pltpu.VMEM((tn, D_MODEL), BF16)]
    has_init = False
    if mode == "conv":
        conv_w, conv_b, hist = conv
        has_init = hist is not None
        in_specs += [pl.BlockSpec((None, CONV_W, tn), lambda j, i: (layer, 0, j)),
                     pl.BlockSpec((None, 1, tn), lambda j, i: (layer, 0, j))]
        args += [conv_w, conv_b]
        if has_init:
            in_specs.append(pl.BlockSpec((None, n_seg, CONV_BUF, tn), lambda j, i: (layer, 0, 0, j)))
            args.append(hist)
        n_seq = (m // tm) * n_seg // tiles_per_seq
        out_specs = [out_specs,
                     pl.BlockSpec((n_seg, CONV_BUF, tn), lambda j, i: (i // tiles_per_seq, 0, j))]
        out_shape = [out_shape, jax.ShapeDtypeStruct((n_seq, CONV_BUF, width), F32)]
        scratch.append(pltpu.VMEM((n_seg, CONV_BUF + tm // n_seg, tn), F32))
    if side is not None:
        _, side_rows, side_cols = side.shape
        slab = side_rows // ((width // tn) * m_tiles)
        in_specs.append(pl.BlockSpec((None, slab, side_cols), lambda j, i: (layer, j * m_tiles + i, 0)))
        args.append(side)
        n_side_tiles = side_cols // side_tn
        out_specs = (out_specs if isinstance(out_specs, list) else [out_specs]) + [
            pl.BlockSpec((n_side_tiles, slab, side_tn), lambda j, i: (0, j * m_tiles + i, 0))]
        out_shape = (out_shape if isinstance(out_shape, list) else [out_shape]) + [
            jax.ShapeDtypeStruct((n_side_tiles, side_rows, side_tn), BF16)]
    n_steps = (width // tn) * m_tiles
    if ring:
        scratch += [pltpu.VMEM((RING_SLOTS, tm, D_MODEL), BF16), pltpu.SemaphoreType.DMA((RING_SLOTS,))]
    return pl.pallas_call(
        functools.partial(_inproj_kernel, mode=mode, n_seg=n_seg, has_init=has_init,
                          tiles_per_seq=tiles_per_seq, sub=256, side_cast=side is not None,
                          ring=ring, tm=tm, m_tiles=m_tiles, n_steps=n_steps),
        grid=(width // tn, m // tm),
        in_specs=in_specs,
        out_specs=out_specs,
        out_shape=out_shape,
        scratch_shapes=scratch,
        compiler_params=pltpu.CompilerParams(
            dimension_semantics=("arbitrary", "arbitrary"), vmem_limit_bytes=VMEM_LIMIT),
        name="in_proj_" + mode,
    )(*args)


def _outproj_kernel(m_ref, w_hbm, x_ref, g_ref, lng_ref, lnb_ref, o_ref, acc_ref, wbuf, wsem, *, nj, tn, sub,
                    n_steps):
    j = pl.program_id(1)
    n_mod = g_ref.shape[0]
    tm = m_ref.shape[0]
    rows_per_mod = tm // n_mod
    w_ref = _ring_fetch(lambda k: w_hbm.at[k % nj], wbuf, wsem, pl.program_id(0) * nj + j, n_steps)
    o = jnp.dot(m_ref[...], w_ref[...], preferred_element_type=F32)
    for r in range(n_mod):
        rs = slice(r * rows_per_mod, (r + 1) * rows_per_mod)
        acc_ref[j, rs, :] = ALPHA * x_ref[rs, :] + g_ref[r:r + 1, :] * o[rs, :]

    @pl.when(j == nj - 1)
    def _():
        inv_d = 1.0 / (nj * tn)
        for s0 in range(0, tm, sub):
            rs = slice(s0, s0 + sub)
            total = acc_ref[0, rs, :].sum(axis=-1, keepdims=True)
            for jj in range(1, nj):
                total = total + acc_ref[jj, rs, :].sum(axis=-1, keepdims=True)
            mu = total * inv_d
            sq = jnp.zeros_like(mu)
            for jj in range(nj):
                d = acc_ref[jj, rs, :] - mu
                sq = sq + (d * d).sum(axis=-1, keepdims=True)
            inv = lax.rsqrt(sq * inv_d + LN_EPS)
            for jj in range(nj):
                cs = slice(jj * tn, (jj + 1) * tn)
                o_ref[rs, cs] = (acc_ref[jj, rs, :] - mu) * inv * lng_ref[:, cs] + lnb_ref[:, cs]


def _outproj(mixed, w_out_b, x2d, gate3, ln_g, ln_b, layer, tm, tn):
    m = x2d.shape[0]
    n_mod = gate3.shape[1]
    tiles_per_mod_block = (m // tm) // gate3.shape[0]
    nj = D_MODEL // tn
    vec_spec = pl.BlockSpec((None, 1, D_MODEL), lambda i, j: (layer, 0, 0))
    return pl.pallas_call(
        functools.partial(_outproj_kernel, nj=nj, tn=tn, sub=min(128, tm), n_steps=(m // tm) * nj),
        grid=(m // tm, nj),
        in_specs=[
            pl.BlockSpec((tm, D_INNER), lambda i, j: (i, 0)),
            pl.BlockSpec(memory_space=pl.ANY),
            pl.BlockSpec((tm, tn), lambda i, j: (i, j)),
            pl.BlockSpec((None, n_mod, tn), lambda i, j: (i // tiles_per_mod_block, 0, j)),
            vec_spec,
            vec_spec,
        ],
        out_specs=pl.BlockSpec((tm, D_MODEL), lambda i, j: (i, 0)),
        out_shape=jax.ShapeDtypeStruct((m, D_MODEL), F32),
        scratch_shapes=[pltpu.VMEM((nj, tm, tn), F32), pltpu.VMEM((RING_SLOTS, D_INNER, tn), BF16),
                        pltpu.SemaphoreType.DMA((RING_SLOTS,))],
        compiler_params=pltpu.CompilerParams(
            dimension_semantics=("arbitrary", "arbitrary"), vmem_limit_bytes=VMEM_LIMIT),
        name="out_proj",
    )(mixed, w_out_b, x2d, gate3, ln_g.reshape(DEPTH, 1, D_MODEL), ln_b.reshape(DEPTH, 1, D_MODEL))


def _split3(a):
    hi = a.astype(BF16)
    r1 = a - hi.astype(F32)
    mid = r1.astype(BF16)
    lo = (r1 - mid.astype(F32)).astype(BF16)
    return hi, mid, lo


def _mixer_kernel(*refs, tv, nchunks, start_pos, has_init, has_stack, layer):
    if has_init:
        (u_ref, gz_ref, xc_ref, dtr_ref, pool0_ref, ssm0_ref, *rest) = refs
    else:
        (u_ref, gz_ref, xc_ref, dtr_ref, *rest) = refs
    (wpool_ref, pscale_ref, dtb_ref, alog_ref, dskip_ref, normw_ref, *rest) = rest
    if has_stack:
        rest = rest[1:]
    mixed_ref, npool_ref, nssm_ref, poolbuf, winbuf, h_ref = rest
    t = MIX_T
    c = pl.program_id(1)

    def pad_rows(v):
        if tv == t:
            return v
        return jnp.concatenate([v, jnp.zeros((t - tv, v.shape[1]), v.dtype)], axis=0)

    @pl.when(c == 0)
    def _init():
        if has_init:
            poolbuf[0:POOL_BUF, :] = pool0_ref[...]
            for g in range(GROUPS):
                for j in range(PAIRS_PER_GROUP):
                    r0 = (g * HPG + 2 * j) * HEAD_DIM
                    h_ref[g, :, j * LANES:(j + 1) * LANES] = ssm0_ref[r0:r0 + LANES, :].T
        else:
            poolbuf[0:POOL_BUF, :] = jnp.zeros((POOL_BUF, W_POOL), F32)
            h_ref[...] = jnp.zeros(h_ref.shape, F32)

    poolbuf[POOL_BUF:POOL_BUF + tv, :] = u_ref[...]
    row = lax.broadcasted_iota(jnp.int32, (tv, 1), 0)
    pos = start_pos + c * tv + row
    for g, w in enumerate(POOL_WINDOWS):
        cols = slice(g * POOL_GW, (g + 1) * POOL_GW)
        u_g = poolbuf[POOL_BUF:POOL_BUF + tv, cols]
        src, d, level = poolbuf, 1, 0
        while 2 * d < w:
            lo = POOL_BUF - (w - 2 * d)
            dst = winbuf.at[level % 2]
            dst[lo:POOL_BUF + tv, cols] = src[lo:POOL_BUF + tv, cols] + src[lo - d:POOL_BUF + tv - d, cols]
            src, d, level = dst, 2 * d, level + 1
        s = src[POOL_BUF:POOL_BUF + tv, cols] + src[POOL_BUF - d:POOL_BUF + tv - d, cols]
        cnt = jnp.minimum(pos + 1, w).astype(F32)
        pooled = (s / cnt - u_g).astype(BF16)
        po = jnp.dot(pooled, wpool_ref[g], preferred_element_type=F32)
        mixed_ref[:, cols] = (po * pscale_ref[:, cols] * gz_ref[:, cols]).astype(BF16)
    poolbuf[0:POOL_BUF, :] = poolbuf[tv:tv + POOL_BUF, :]

    x_dt = dtr_ref[...] + dtb_ref[...]
    dt = jnp.maximum(x_dt, 0.0) + jnp.log1p(jnp.exp(-jnp.abs(x_dt)))
    if tv < t:
        dt = jnp.concatenate([dt, jnp.zeros((t - tv, LANES), F32)], axis=0)
    a = dt * (-LOG2E * jnp.exp(alog_ref[...]))
    ri = lax.broadcasted_iota(jnp.int32, (t, t), 0)
    ci = lax.broadcasted_iota(jnp.int32, (t, t), 1)
    tril = ri >= ci
    tri = jnp.where(tril, 1.0, 0.0).astype(BF16)
    a_cs = sum(jnp.dot(tri, piece, preferred_element_type=F32) for piece in _split3(a))
    a_last = a_cs[t - 1:t, :]
    ea_last = jnp.exp2(a_last)
    a_cs_t = a_cs.T
    dt_t = dt.T
    w_t = (jnp.exp2(a_last - a_cs) * dt).T
    lo_half = lax.broadcasted_iota(jnp.int32, (1, LANES), 1) < HEAD_DIM

    def block_diag(v):
        zero = jnp.zeros_like(v)
        return jnp.concatenate([jnp.where(lo_half, v, zero), jnp.where(lo_half, zero, v)], axis=0)

    for g in range(GROUPS):
        b_g = pad_rows(xc_ref[:, W_SSD + g * STATE:W_SSD + (g + 1) * STATE])
        c_g = pad_rows(xc_ref[:, W_SSD + GROUPS * STATE + g * STATE:W_SSD + GROUPS * STATE + (g + 1) * STATE])
        cb = lax.dot_general(c_g.astype(BF16), b_g.astype(BF16), (((1,), (1,)), ((), ())),
                             preferred_element_type=F32)
        b_t = b_g.T
        ys = []
        for j in range(PAIRS_PER_GROUP):
            p = g * PAIRS_PER_GROUP + j
            pc = slice(p * LANES, (p + 1) * LANES)
            jc = slice(j * LANES, (j + 1) * LANES)
            xs_p = pad_rows(xc_ref[:, pc])
            h_p = h_ref[g, :, jc]
            bd_x = block_diag(xs_p.astype(BF16))
            bd_h = block_diag(h_p.astype(BF16))
            m_parts, e_parts, b_parts = [], [], []
            for k in (2 * p, 2 * p + 1):
                col = jnp.broadcast_to(a_cs[0:tv, k:k + 1], (tv, t))
                lmat = jnp.where(tril[0:tv], jnp.exp2(col - a_cs_t[k:k + 1, :]), 0.0)
                m_parts.append((cb[0:tv] * lmat * dt_t[k:k + 1, :]).astype(BF16))
                e_parts.append((c_g[0:tv] * jnp.exp2(col)).astype(BF16))
                b_parts.append((b_t * w_t[k:k + 1, :]).astype(BF16))
            y_mm = jnp.dot(jnp.concatenate(m_parts + e_parts, axis=1),
                           jnp.concatenate([bd_x, bd_h], axis=0), preferred_element_type=F32)
            ys.append(y_mm + dskip_ref[:, pc] * xs_p[0:tv])
            st = jnp.dot(jnp.concatenate(b_parts, axis=1), bd_x, preferred_element_type=F32)
            cd = jnp.where(lo_half, ea_last[:, 2 * p:2 * p + 1], ea_last[:, 2 * p + 1:2 * p + 2])
            h_ref[g, :, jc] = cd * h_p + st
        gc = slice(g * GROUP_W, (g + 1) * GROUP_W)
        y_g = jnp.concatenate(ys, axis=1)
        y_g = y_g * gz_ref[:, W_POOL + g * GROUP_W:W_POOL + (g + 1) * GROUP_W]
        ms = jnp.mean(y_g * y_g, axis=-1, keepdims=True)
        out = y_g * lax.rsqrt(ms + RMS_EPS) * normw_ref[:, gc]
        mixed_ref[:, W_POOL + g * GROUP_W:W_POOL + (g + 1) * GROUP_W] = out.astype(BF16)

    @pl.when(c == nchunks - 1)
    def _fin():
        npool_ref[...] = poolbuf[0:POOL_BUF, :]
        if has_stack:
            state_out = nssm_ref
        else:
            state_out = nssm_ref.at[layer]
            for other in range(DEPTH):
                if other != layer:
                    nssm_ref[other] = jnp.zeros((W_SSD, STATE), F32)
        for g in range(GROUPS):
            for j in range(PAIRS_PER_GROUP):
                r0 = (g * HPG + 2 * j) * HEAD_DIM
                state_out[r0:r0 + LANES, :] = h_ref[g, :, j * LANES:(j + 1) * LANES].T


def _mixer(u, gz, xc, dt_raw, init, ssm_stack, params, layer, nb, seq, start_pos):
    tv = min(seq, MIX_T)
    nchunks = seq // tv
    has_init = init is not None
    has_stack = ssm_stack is not None
    row_map = lambda b, c: (b * nchunks + c, 0)
    per_layer2 = lambda b, c: (layer, 0, 0)
    per_b = lambda b, c: (b, 0, 0)
    per_layer_b = lambda b, c: (layer, b, 0, 0)
    in_specs = [pl.BlockSpec((tv, W_POOL), row_map), pl.BlockSpec((tv, W_POOL + W_SSD), row_map),
                pl.BlockSpec((tv, CONV_DIM), row_map), pl.BlockSpec((tv, LANES), row_map)]
    args = [u, gz, xc, dt_raw]
    if has_init:
        in_specs += [pl.BlockSpec((None, None, POOL_BUF, W_POOL), per_layer_b),
                     pl.BlockSpec((None, None, W_SSD, STATE), per_layer_b)]
        args += list(init)
    in_specs += [
        pl.BlockSpec((None, len(POOL_WINDOWS), POOL_GW, POOL_GW), lambda b, c: (layer, 0, 0, 0)),
        pl.BlockSpec((None, 1, W_POOL), per_layer2),
        pl.BlockSpec((None, 1, LANES), per_layer2),
        pl.BlockSpec((None, 1, LANES), per_layer2),
        pl.BlockSpec((None, 1, W_SSD), per_layer2),
        pl.BlockSpec((None, 1, W_SSD), per_layer2),
    ]
    args += list(params)
    aliases = {}
    if has_stack:
        in_specs.append(pl.BlockSpec(memory_space=pl.ANY))
        args.append(ssm_stack)
        aliases = {len(args) - 1: 2}
    return pl.pallas_call(
        functools.partial(_mixer_kernel, tv=tv, nchunks=nchunks, start_pos=start_pos, has_init=has_init,
                          has_stack=has_stack, layer=layer),
        grid=(nb, nchunks),
        in_specs=in_specs,
        out_specs=[
            pl.BlockSpec((tv, D_INNER), row_map),
            pl.BlockSpec((None, POOL_BUF, W_POOL), per_b),
            (pl.BlockSpec((None, None, W_SSD, STATE), per_layer_b) if has_stack
             else pl.BlockSpec((DEPTH, None, W_SSD, STATE), lambda b, c: (0, b, 0, 0))),
        ],
        out_shape=[
            jax.ShapeDtypeStruct((nb * seq, D_INNER), BF16),
            jax.ShapeDtypeStruct((nb, POOL_BUF, W_POOL), F32),
            jax.ShapeDtypeStruct((DEPTH, nb, W_SSD, STATE), F32),
        ],
        input_output_aliases=aliases,
        scratch_shapes=[
            pltpu.VMEM((POOL_BUF + tv, W_POOL), F32),
            pltpu.VMEM((2, POOL_BUF + tv, W_POOL), F32),
            pltpu.VMEM((GROUPS, STATE, GROUP_W), F32),
        ],
        compiler_params=pltpu.CompilerParams(
            dimension_semantics=("parallel", "arbitrary"), vmem_limit_bytes=VMEM_LIMIT),
        name="mixer_init" if has_init else "mixer_zero",
    )(*args)


def _pad_rows_front(a, rows):
    pad = rows - a.shape[-2]
    return jnp.pad(a, [(0, 0)] * (a.ndim - 2) + [(pad, 0), (0, 0)])


def _pad_lanes(a):
    return jnp.pad(a, [(0, 0)] * (a.ndim - 1) + [(0, LANES - a.shape[-1])])


def kernel(x_prompt, x_sample, state_pool, state_conv, state_ssm, c_prompt, c_sample,
           w_ada, b_ada, w_in, w_pool, pool_scale, conv_w, conv_b,
           dt_bias, a_log, d_skip, ssd_norm_w, w_out, ln_g, ln_b):
    nbp, seq_p, _ = x_prompt.shape
    nbs, seq_s, _ = x_sample.shape
    past_len = 1024

    n_c = nbp + nbs
    c_rows = -(-n_c // 8) * 8
    c_all = jnp.concatenate([c_prompt, c_sample, jnp.zeros((c_rows - n_c, D_MODEL), F32)], axis=0)
    mod = _ada(c_all, w_ada, b_ada)

    w_in_t = jnp.swapaxes(w_in, 1, 2)
    w_dt_b = _pad_lanes(w_in[:, :, MAIN_DIM:]).astype(BF16)
    conv_b3 = conv_b[:, None]
    params = (w_pool.astype(BF16), pool_scale[:, None],
              _pad_lanes(dt_bias)[:, None], _pad_lanes(a_log)[:, None],
              jnp.repeat(d_skip, HEAD_DIM, axis=-1)[:, None], ssd_norm_w[:, None])
    conv0 = _pad_rows_front(state_conv, CONV_BUF)
    init = (_pad_rows_front(state_pool, POOL_BUF),
            state_ssm.reshape(DEPTH, nbs, W_SSD, STATE))

    def in_projections(h, l, tm, hist, n_seg, tiles_per_seq, side=None):
        tm_plain = min(h.shape[0], 1024)
        u = _inproj(h, w_in_t, l, OFF_U, W_POOL, "raw", tm_plain, 512, ring=True)
        gz = _inproj(h, w_in_t, l, OFF_G, W_POOL + W_SSD, "silu", tm_plain, 512, ring=True)
        xc, nconv, *side_b = _inproj(h, w_in_t, l, OFF_XBC, CONV_DIM, "conv", tm, 512, side=side,
                                     side_tn=OUT_TN, conv=(conv_w, conv_b3, hist), n_seg=n_seg,
                                     tiles_per_seq=tiles_per_seq)
        return (u, gz, xc, nconv, *side_b)

    xp = x_prompt.reshape(nbp * seq_p, D_MODEL)
    xs = x_sample.reshape(nbs * seq_s, D_MODEL)
    outs = {k: [] for k in ("pp", "pc", "psm", "sp", "sc", "ssm")}
    tm_p = 1024
    ssm_p = ssm_s = None
    for l in range(DEPTH):
        shift, scale, gate = (mod[l, :, i * D_MODEL:(i + 1) * D_MODEL] for i in range(3))

        sc3, sh3, g3 = (v[0:nbp].reshape(nbp, 1, D_MODEL) for v in (scale, shift, gate))
        h, dt_raw = _lnmod(xp, sc3, sh3, w_dt_b, l, tm=512)
        u, gz, xc, nconv, w_out_b = in_projections(h, l, tm_p, None, 1, seq_p // tm_p, side=w_out)
        mixed, npool, ssm_p = _mixer(u, gz, xc, dt_raw, None, ssm_p, params, l, nbp, seq_p, 0)
        xp = _outproj(mixed, w_out_b, xp, g3, ln_g, ln_b, l, tm=512, tn=OUT_TN)
        outs["pp"].append(npool[:, 1:])
        outs["pc"].append(nconv[:, CONV_BUF - (CONV_W - 1):])

        sc3, sh3, g3 = (v[nbp:n_c].reshape(1, nbs, D_MODEL) for v in (scale, shift, gate))
        h, dt_raw = _lnmod(xs, sc3, sh3, w_dt_b, l, tm=nbs * seq_s)
        u, gz, xc, nconv = in_projections(h, l, nbs * seq_s, conv0, nbs, 1)
        mixed, npool, ssm_s = _mixer(u, gz, xc, dt_raw, init, ssm_s, params, l, nbs, seq_s, past_len)
        xs = _outproj(mixed, w_out_b, xs, g3, ln_g, ln_b, l, tm=nbs * seq_s, tn=OUT_TN)
        outs["sp"].append(npool[:, 1:])
        outs["sc"].append(nconv[:, CONV_BUF - (CONV_W - 1):])

    return (xp.reshape(nbp, seq_p, D_MODEL), xs.reshape(nbs, seq_s, D_MODEL),
            jnp.stack(outs["pp"]), jnp.stack(outs["pc"]), ssm_p.reshape(DEPTH, nbp, HEADS, HEAD_DIM, STATE),
            jnp.stack(outs["sp"]), jnp.stack(outs["sc"]), ssm_s.reshape(DEPTH, nbs, HEADS, HEAD_DIM, STATE))
```

```python
import functools
import math

import jax
import jax.numpy as jnp
from jax import lax
from jax.experimental import pallas as pl
from jax.experimental.pallas import tpu as pltpu

F32 = jnp.float32
BF16 = jnp.bfloat16

D_MODEL = 4096
DEPTH = 2
D_INNER = 2 * D_MODEL
W_POOL = D_INNER // 4
POOL_WINDOWS = (2, 4, 8, 16)
POOL_GW = W_POOL // len(POOL_WINDOWS)
POOL_HIST = max(POOL_WINDOWS) - 1
W_SSD = D_INNER - W_POOL
HEAD_DIM = 64
HEADS = W_SSD // HEAD_DIM
GROUPS = 8
HPG = HEADS // GROUPS
GROUP_W = HPG * HEAD_DIM
STATE = 128
CONV_W = 4
CONV_DIM = W_SSD + 2 * GROUPS * STATE
MAIN_DIM = 2 * W_POOL + W_SSD + CONV_DIM
ALPHA = (2 * DEPTH) ** 0.25
LN_EPS = 1e-5
RMS_EPS = 1e-5
LOG2E = math.log2(math.e)

LANES = 128
POOL_BUF = 16
CONV_BUF = 8
MIX_T = 128
PAIRS_PER_GROUP = GROUP_W // LANES
assert MIX_T == STATE == LANES and 2 * HEAD_DIM == LANES
VMEM_LIMIT = 56 * 1024 * 1024
RING_SLOTS = 3
OUT_TN = 256

OFF_U, OFF_G, OFF_Z, OFF_XBC = 0, W_POOL, 2 * W_POOL, 2 * W_POOL + W_SSD


def _silu_of_double(h):
    return h * jnp.tanh(h) + h


def _silu(x):
    return _silu_of_double(0.5 * x)


def _layer_norm(x):
    mu = jnp.mean(x, axis=-1, keepdims=True)
    xc = x - mu
    var = jnp.mean(xc * xc, axis=-1, keepdims=True)
    return xc * lax.rsqrt(var + LN_EPS)


def _ada_kernel(c_ref, w_ref, b_ref, o_ref):
    sc = _silu(c_ref[...]).astype(BF16)
    o_ref[...] = jnp.dot(sc, w_ref[...].astype(BF16), preferred_element_type=F32) + b_ref[...]


def _ada(c_all, w_ada, b_ada, tn=512):
    rows = c_all.shape[0]
    n = w_ada.shape[-1]
    return pl.pallas_call(
        _ada_kernel,
        grid=(DEPTH, n // tn),
        in_specs=[
            pl.BlockSpec((rows, D_MODEL), lambda l, j: (0, 0)),
            pl.BlockSpec((None, D_MODEL, tn), lambda l, j: (l, 0, j)),
            pl.BlockSpec((None, 1, tn), lambda l, j: (l, 0, j)),
        ],
        out_specs=pl.BlockSpec((None, rows, tn), lambda l, j: (l, 0, j)),
        out_shape=jax.ShapeDtypeStruct((DEPTH, rows, n), F32),
        compiler_params=pltpu.CompilerParams(
            dimension_semantics=("arbitrary", "arbitrary"), vmem_limit_bytes=VMEM_LIMIT),
        name="ada_mod",
    )(c_all, w_ada, b_ada.reshape(DEPTH, 1, n))


def _lnmod_kernel(x_ref, sc_ref, sh_ref, wdt_ref, h_ref, dt_ref, *, sub):
    n_mod = sc_ref.shape[0]
    rows_per_mod = x_ref.shape[0] // n_mod
    step = min(sub, rows_per_mod)
    for r in range(n_mod):
        for s0 in range(0, rows_per_mod, step):
            rs = slice(r * rows_per_mod + s0, r * rows_per_mod + s0 + step)
            h = _layer_norm(x_ref[rs, :]) * (1.0 + sc_ref[r:r + 1, :]) + sh_ref[r:r + 1, :]
            h_ref[rs, :] = h.astype(BF16)
    dt_ref[...] = jnp.dot(h_ref[...], wdt_ref[...], preferred_element_type=F32)


def _lnmod(x2d, scale3, shift3, w_dt_b, layer, tm):
    m = x2d.shape[0]
    n_mod = scale3.shape[1]
    tiles_per_mod_block = (m // tm) // scale3.shape[0]
    mod_spec = pl.BlockSpec((None, n_mod, D_MODEL), lambda i: (i // tiles_per_mod_block, 0, 0))
    return pl.pallas_call(
        functools.partial(_lnmod_kernel, sub=128),
        grid=(m // tm,),
        in_specs=[
            pl.BlockSpec((tm, D_MODEL), lambda i: (i, 0)),
            mod_spec,
            mod_spec,
            pl.BlockSpec((None, D_MODEL, LANES), lambda i: (layer, 0, 0)),
        ],
        out_specs=[
            pl.BlockSpec((tm, D_MODEL), lambda i: (i, 0)),
            pl.BlockSpec((tm, LANES), lambda i: (i, 0)),
        ],
        out_shape=[
            jax.ShapeDtypeStruct((m, D_MODEL), BF16),
            jax.ShapeDtypeStruct((m, LANES), F32),
        ],
        compiler_params=pltpu.CompilerParams(
            dimension_semantics=("parallel",), vmem_limit_bytes=VMEM_LIMIT),
        name="ln_mod",
    )(x2d, scale3, shift3, w_dt_b)


def _ring_fetch(src_tile, bufs, sems, step, n_steps):
    slots = bufs.shape[0]

    def copy(k):
        return pltpu.make_async_copy(src_tile(k), bufs.at[k % slots], sems.at[k % slots])

    @pl.when(step == 0)
    def _():
        for k in range(min(slots - 1, n_steps)):
            copy(k).start()

    @pl.when(step + (slots - 1) < n_steps)
    def _():
        copy(step + (slots - 1)).start()

    copy(step).wait()
    return bufs.at[step % slots]


def _inproj_kernel(*refs, mode, n_seg, has_init, tiles_per_seq, sub, side_cast, ring, tm, m_tiles, n_steps):
    refs = list(refs)
    if ring:
        hsem = refs.pop()
        hbuf = refs.pop()
    if side_cast:
        n_in = {"conv": 5 if has_init else 4}.get(mode, 2)
        side_ref = refs.pop(n_in)
        side_b_ref = refs.pop(n_in + (2 if mode == "conv" else 1))
        for jj in range(side_b_ref.shape[0]):
            cw = side_b_ref.shape[2]
            side_b_ref[jj] = side_ref[:, jj * cw:(jj + 1) * cw].astype(BF16)
    if mode == "conv":
        if has_init:
            h_ref, w_ref, cw_ref, cb_ref, hist_ref, o_ref, st_ref, wb_ref, buf = refs
        else:
            h_ref, w_ref, cw_ref, cb_ref, o_ref, st_ref, wb_ref, buf = refs
    else:
        h_ref, w_ref, o_ref, wb_ref = refs
    i = pl.program_id(1)
    if ring:
        h_hbm = h_ref

        def h_rows(k):
            row0 = (k % m_tiles) * tm
            return h_hbm.at[pl.ds(row0 if isinstance(row0, int) else pl.multiple_of(row0, tm), tm)]

        h_ref = _ring_fetch(h_rows, hbuf, hsem, pl.program_id(0) * m_tiles + i, n_steps)

    @pl.when(i == 0)
    def _():
        wb_ref[...] = w_ref[...].astype(BF16)

    def project(r0, n):
        return lax.dot_general(h_ref[r0:r0 + n, :], wb_ref[...], (((1,), (1,)), ((), ())),
                               preferred_element_type=F32)

    if mode == "raw":
        o_ref[...] = project(0, tm)
    elif mode == "silu":
        o_ref[...] = _silu(project(0, tm))
    else:
        seg = tm // n_seg
        half = tm // 2 if (n_seg == 1 and tm >= 2 * sub) else tm
        if not has_init:
            @pl.when(i % tiles_per_seq == 0)
            def _():
                buf[0, 0:CONV_BUF, :] = jnp.zeros((CONV_BUF, buf.shape[-1]), F32)
        for s in range(n_seg):
            if has_init:
                buf[s, 0:CONV_BUF, :] = hist_ref[s]
        cw_half = 0.5 * cw_ref[...]
        cb_half = 0.5 * cb_ref[...]
        for h0 in range(0, tm, half):
            res = project(h0, half)
            for s in range(h0 // seg, max(h0 // seg + 1, (h0 + half) // seg)):
                lo = max(h0, s * seg) - s * seg
                hi = min(h0 + half, (s + 1) * seg) - s * seg
                buf[s, CONV_BUF + lo:CONV_BUF + hi, :] = res[s * seg + lo - h0:s * seg + hi - h0, :]
                for r0 in range(lo, hi, sub):
                    n = min(sub, hi - r0)
                    acc = cb_half + buf[s, CONV_BUF + r0:CONV_BUF + r0 + n, :] * cw_half[CONV_W - 1:CONV_W, :]
                    for back in range(1, CONV_W):
                        acc = acc + (buf[s, CONV_BUF + r0 - back:CONV_BUF + r0 - back + n, :]
                                     * cw_half[CONV_W - 1 - back:CONV_W - back, :])
                    o_ref[s * seg + r0:s * seg + r0 + n, :] = _silu_of_double(acc)
        for s in range(n_seg):
            st_ref[s] = buf[s, seg:seg + CONV_BUF, :]
            if not has_init:
                buf[s, 0:CONV_BUF, :] = buf[s, seg:seg + CONV_BUF, :]


def _inproj(h2d, w_in_t, layer, col0, width, mode, tm, tn, conv=None, n_seg=1, tiles_per_seq=1, side=None,
            side_tn=None, ring=False):
    m = h2d.shape[0]
    j0 = col0 // tn
    m_tiles = m // tm
    in_specs = [
        pl.BlockSpec(memory_space=pl.ANY) if ring else pl.BlockSpec((tm, D_MODEL), lambda j, i: (i, 0)),
        pl.BlockSpec((None, tn, D_MODEL), lambda j, i: (layer, j0 + j, 0)),
    ]
    args = [h2d, w_in_t]
    out_specs = pl.BlockSpec((tm, tn), lambda j, i: (i, j))
    out_shape = jax.ShapeDtypeStruct((m, width), F32)
    scratch = [pltpu.VMEM((tn, D_MODEL), BF16)]
    has_init = False
    if mode == "conv":
        conv_w, conv_b, hist = conv
        has_init = hist is not None
        in_specs += [pl.BlockSpec((None, CONV_W, tn), lambda j, i: (layer, 0, j)),
                     pl.BlockSpec((None, 1, tn), lambda j, i: (layer, 0, j))]
        args += [conv_w, conv_b]
        if has_init:
            in_specs.append(pl.BlockSpec((None, n_seg, CONV_BUF, tn), lambda j, i: (layer, 0, 0, j)))
            args.append(hist)
        n_seq = (m // tm) * n_seg // tiles_per_seq
        out_specs = [out_specs,
                     pl.BlockSpec((n_seg, CONV_BUF, tn), lambda j, i: (i // tiles_per_seq, 0, j))]
        out_shape = [out_shape, jax.ShapeDtypeStruct((n_seq, CONV_BUF, width), F32)]
        scratch.append(pltpu.VMEM((n_seg, CONV_BUF + tm // n_seg, tn), F32))
    if side is not None:
        _, side_rows, side_cols = side.shape
        slab = side_rows // ((width // tn) * m_tiles)
        in_specs.append(pl.BlockSpec((None, slab, side_cols), lambda j, i: (layer, j * m_tiles + i, 0)))
        args.append(side)
        n_side_tiles = side_cols // side_tn
        out_specs = (out_specs if isinstance(out_specs, list) else [out_specs]) + [
            pl.BlockSpec((n_side_tiles, slab, side_tn), lambda j, i: (0, j * m_tiles + i, 0))]
        out_shape = (out_shape if isinstance(out_shape, list) else [out_shape]) + [
            jax.ShapeDtypeStruct((n_side_tiles, side_rows, side_tn), BF16)]
    n_steps = (width // tn) * m_tiles
    if ring:
        scratch += [pltpu.VMEM((RING_SLOTS, tm, D_MODEL), BF16), pltpu.SemaphoreType.DMA((RING_SLOTS,))]
    return pl.pallas_call(
        functools.partial(_inproj_kernel, mode=mode, n_seg=n_seg, has_init=has_init,
                          tiles_per_seq=tiles_per_seq, sub=256, side_cast=side is not None,
                          ring=ring, tm=tm, m_tiles=m_tiles, n_steps=n_steps),
        grid=(width // tn, m // tm),
        in_specs=in_specs,
        out_specs=out_specs,
        out_shape=out_shape,
        scratch_shapes=scratch,
        compiler_params=pltpu.CompilerParams(
            dimension_semantics=("arbitrary", "arbitrary"), vmem_limit_bytes=VMEM_LIMIT),
        name="in_proj_" + mode,
    )(*args)


def _outproj_kernel(m_ref, w_hbm, x_ref, g_ref, lng_ref, lnb_ref, o_ref, acc_ref, wbuf, wsem, *, nj, tn, sub,
                    n_steps):
    j = pl.program_id(1)
    n_mod = g_ref.shape[0]
    tm = m_ref.shape[0]
    rows_per_mod = tm // n_mod
    w_ref = _ring_fetch(lambda k: w_hbm.at[k % nj], wbuf, wsem, pl.program_id(0) * nj + j, n_steps)
    o = jnp.dot(m_ref[...], w_ref[...], preferred_element_type=F32)
    for r in range(n_mod):
        rs = slice(r * rows_per_mod, (r + 1) * rows_per_mod)
        acc_ref[j, rs, :] = ALPHA * x_ref[rs, :] + g_ref[r:r + 1, :] * o[rs, :]

    @pl.when(j == nj - 1)
    def _():
        inv_d = 1.0 / (nj * tn)
        for s0 in range(0, tm, sub):
            rs = slice(s0, s0 + sub)
            total = acc_ref[0, rs, :].sum(axis=-1, keepdims=True)
            for jj in range(1, nj):
                total = total + acc_ref[jj, rs, :].sum(axis=-1, keepdims=True)
            mu = total * inv_d
            sq = jnp.zeros_like(mu)
            for jj in range(nj):
                d = acc_ref[jj, rs, :] - mu
                sq = sq + (d * d).sum(axis=-1, keepdims=True)
            inv = lax.rsqrt(sq * inv_d + LN_EPS)
            for jj in range(nj):
                cs = slice(jj * tn, (jj + 1) * tn)
                o_ref[rs, cs] = (acc_ref[jj, rs, :] - mu) * inv * lng_ref[:, cs] + lnb_ref[:, cs]


def _outproj(mixed, w_out_b, x2d, gate3, ln_g, ln_b, layer, tm, tn):
    m = x2d.shape[0]
    n_mod = gate3.shape[1]
    tiles_per_mod_block = (m // tm) // gate3.shape[0]
    nj = D_MODEL // tn
    vec_spec = pl.BlockSpec((None, 1, D_MODEL), lambda i, j: (layer, 0, 0))
    return pl.pallas_call(
        functools.partial(_outproj_kernel, nj=nj, tn=tn, sub=min(128, tm), n_steps=(m // tm) * nj),
        grid=(m // tm, nj),
        in_specs=[
            pl.BlockSpec((tm, D_INNER), lambda i, j: (i, 0)),
            pl.BlockSpec(memory_space=pl.ANY),
            pl.BlockSpec((tm, tn), lambda i, j: (i, j)),
            pl.BlockSpec((None, n_mod, tn), lambda i, j: (i // tiles_per_mod_block, 0, j)),
            vec_spec,
            vec_spec,
        ],
        out_specs=pl.BlockSpec((tm, D_MODEL), lambda i, j: (i, 0)),
        out_shape=jax.ShapeDtypeStruct((m, D_MODEL), F32),
        scratch_shapes=[pltpu.VMEM((nj, tm, tn), F32), pltpu.VMEM((RING_SLOTS, D_INNER, tn), BF16),
                        pltpu.SemaphoreType.DMA((RING_SLOTS,))],
        compiler_params=pltpu.CompilerParams(
            dimension_semantics=("arbitrary", "arbitrary"), vmem_limit_bytes=VMEM_LIMIT),
        name="out_proj",
    )(mixed, w_out_b, x2d, gate3, ln_g.reshape(DEPTH, 1, D_MODEL), ln_b.reshape(DEPTH, 1, D_MODEL))


def _split3(a):
    hi = a.astype(BF16)
    r1 = a - hi.astype(F32)
    mid = r1.astype(BF16)
    lo = (r1 - mid.astype(F32)).astype(BF16)
    return hi, mid, lo


def _mixer_kernel(*refs, tv, nchunks, start_pos, has_init, has_stack, layer):
    if has_init:
        (u_ref, gz_ref, xc_ref, dtr_ref, pool0_ref, ssm0_ref, *rest) = refs
    else:
        (u_ref, gz_ref, xc_ref, dtr_ref, *rest) = refs
    (wpool_ref, pscale_ref, dtb_ref, alog_ref, dskip_ref, normw_ref, *rest) = rest
    if has_stack:
        rest = rest[1:]
    mixed_ref, npool_ref, nssm_ref, poolbuf, winbuf, h_ref = rest
    t = MIX_T
    c = pl.program_id(1)

    def pad_rows(v):
        if tv == t:
            return v
        return jnp.concatenate([v, jnp.zeros((t - tv, v.shape[1]), v.dtype)], axis=0)

    @pl.when(c == 0)
    def _init():
        if has_init:
            poolbuf[0:POOL_BUF, :] = pool0_ref[...]
            for g in range(GROUPS):
                for j in range(PAIRS_PER_GROUP):
                    r0 = (g * HPG + 2 * j) * HEAD_DIM
                    h_ref[g, :, j * LANES:(j + 1) * LANES] = ssm0_ref[r0:r0 + LANES, :].T
        else:
            poolbuf[0:POOL_BUF, :] = jnp.zeros((POOL_BUF, W_POOL), F32)
            h_ref[...] = jnp.zeros(h_ref.shape, F32)

    poolbuf[POOL_BUF:POOL_BUF + tv, :] = u_ref[...]
    row = lax.broadcasted_iota(jnp.int32, (tv, 1), 0)
    pos = start_pos + c * tv + row
    for g, w in enumerate(POOL_WINDOWS):
        cols = slice(g * POOL_GW, (g + 1) * POOL_GW)
        u_g = poolbuf[POOL_BUF:POOL_BUF + tv, cols]
        src, d, level = poolbuf, 1, 0
        while 2 * d < w:
            lo = POOL_BUF - (w - 2 * d)
            dst = winbuf.at[level % 2]
            dst[lo:POOL_BUF + tv, cols] = src[lo:POOL_BUF + tv, cols] + src[lo - d:POOL_BUF + tv - d, cols]
            src, d, level = dst, 2 * d, level + 1
        s = src[POOL_BUF:POOL_BUF + tv, cols] + src[POOL_BUF - d:POOL_BUF + tv - d, cols]
        cnt = jnp.minimum(pos + 1, w).astype(F32)
        pooled = (s / cnt - u_g).astype(BF16)
        po = jnp.dot(pooled, wpool_ref[g], preferred_element_type=F32)
        mixed_ref[:, cols] = (po * pscale_ref[:, cols] * gz_ref[:, cols]).astype(BF16)
    poolbuf[0:POOL_BUF, :] = poolbuf[tv:tv + POOL_BUF, :]

    x_dt = dtr_ref[...] + dtb_ref[...]
    dt = jnp.maximum(x_dt, 0.0) + jnp.log1p(jnp.exp(-jnp.abs(x_dt)))
    if tv < t:
        dt = jnp.concatenate([dt, jnp.zeros((t - tv, LANES), F32)], axis=0)
    a = dt * (-LOG2E * jnp.exp(alog_ref[...]))
    ri = lax.broadcasted_iota(jnp.int32, (t, t), 0)
    ci = lax.broadcasted_iota(jnp.int32, (t, t), 1)
    tril = ri >= ci
    tri = jnp.where(tril, 1.0, 0.0).astype(BF16)
    a_cs = sum(jnp.dot(tri, piece, preferred_element_type=F32) for piece in _split3(a))
    a_last = a_cs[t - 1:t, :]
    ea_last = jnp.exp2(a_last)
    a_cs_t = a_cs.T
    dt_t = dt.T
    w_t = (jnp.exp2(a_last - a_cs) * dt).T
    lo_half = lax.broadcasted_iota(jnp.int32, (1, LANES), 1) < HEAD_DIM

    def block_diag(v):
        zero = jnp.zeros_like(v)
        return jnp.concatenate([jnp.where(lo_half, v, zero), jnp.where(lo_half, zero, v)], axis=0)

    for g in range(GROUPS):
        b_g = pad_rows(xc_ref[:, W_SSD + g * STATE:W_SSD + (g + 1) * STATE])
        c_g = pad_rows(xc_ref[:, W_SSD + GROUPS * STATE + g * STATE:W_SSD + GROUPS * STATE + (g + 1) * STATE])
        cb = lax.dot_general(c_g.astype(BF16), b_g.astype(BF16), (((1,), (1,)), ((), ())),
                             preferred_element_type=F32)
        b_t = b_g.T
        ys = []
        for j in range(PAIRS_PER_GROUP):
            p = g * PAIRS_PER_GROUP + j
            pc = slice(p * LANES, (p + 1) * LANES)
            jc = slice(j * LANES, (j + 1) * LANES)
            xs_p = pad_rows(xc_ref[:, pc])
            h_p = h_ref[g, :, jc]
            bd_x = block_diag(xs_p.astype(BF16))
            bd_h = block_diag(h_p.astype(BF16))
            m_parts, e_parts, b_parts = [], [], []
            for k in (2 * p, 2 * p + 1):
                col = jnp.broadcast_to(a_cs[0:tv, k:k + 1], (tv, t))
                lmat = jnp.where(tril[0:tv], jnp.exp2(col - a_cs_t[k:k + 1, :]), 0.0)
                m_parts.append((cb[0:tv] * lmat * dt_t[k:k + 1, :]).astype(BF16))
                e_parts.append((c_g[0:tv] * jnp.exp2(col)).astype(BF16))
                b_parts.append((b_t * w_t[k:k + 1, :]).astype(BF16))
            y_mm = jnp.dot(jnp.concatenate(m_parts + e_parts, axis=1),
                           jnp.concatenate([bd_x, bd_h], axis=0), preferred_element_type=F32)
            ys.append(y_mm + dskip_ref[:, pc] * xs_p[0:tv])
            st = jnp.dot(jnp.concatenate(b_parts, axis=1), bd_x, preferred_element_type=F32)
            cd = jnp.where(lo_half, ea_last[:, 2 * p:2 * p + 1], ea_last[:, 2 * p + 1:2 * p + 2])
            h_ref[g, :, jc] = cd * h_p + st
        gc = slice(g * GROUP_W, (g + 1) * GROUP_W)
        y_g = jnp.concatenate(ys, axis=1)
        y_g = y_g * gz_ref[:, W_POOL + g * GROUP_W:W_POOL + (g + 1) * GROUP_W]
        ms = jnp.mean(y_g * y_g, axis=-1, keepdims=True)
        out = y_g * lax.rsqrt(ms + RMS_EPS) * normw_ref[:, gc]
        mixed_ref[:, W_POOL + g * GROUP_W:W_POOL + (g + 1) * GROUP_W] = out.astype(BF16)

    @pl.when(c == nchunks - 1)
    def _fin():
        npool_ref[...] = poolbuf[0:POOL_BUF, :]
        if has_stack:
            state_out = nssm_ref
        else:
            state_out = nssm_ref.at[layer]
            for other in range(DEPTH):
                if other != layer:
                    nssm_ref[other] = jnp.zeros((W_SSD, STATE), F32)
        for g in range(GROUPS):
            for j in range(PAIRS_PER_GROUP):
                r0 = (g * HPG + 2 * j) * HEAD_DIM
                state_out[r0:r0 + LANES, :] = h_ref[g, :, j * LANES:(j + 1) * LANES].T


def _mixer(u, gz, xc, dt_raw, init, ssm_stack, params, layer, nb, seq, start_pos):
    tv = min(seq, MIX_T)
    nchunks = seq // tv
    has_init = init is not None
    has_stack = ssm_stack is not None
    row_map = lambda b, c: (b * nchunks + c, 0)
    per_layer2 = lambda b, c: (layer, 0, 0)
    per_b = lambda b, c: (b, 0, 0)
    per_layer_b = lambda b, c: (layer, b, 0, 0)
    in_specs = [pl.BlockSpec((tv, W_POOL), row_map), pl.BlockSpec((tv, W_POOL + W_SSD), row_map),
                pl.BlockSpec((tv, CONV_DIM), row_map), pl.BlockSpec((tv, LANES), row_map)]
    args = [u, gz, xc, dt_raw]
    if has_init:
        in_specs += [pl.BlockSpec((None, None, POOL_BUF, W_POOL), per_layer_b),
                     pl.BlockSpec((None, None, W_SSD, STATE), per_layer_b)]
        args += list(init)
    in_specs += [
        pl.BlockSpec((None, len(POOL_WINDOWS), POOL_GW, POOL_GW), lambda b, c: (layer, 0, 0, 0)),
        pl.BlockSpec((None, 1, W_POOL), per_layer2),
        pl.BlockSpec((None, 1, LANES), per_layer2),
        pl.BlockSpec((None, 1, LANES), per_layer2),
        pl.BlockSpec((None, 1, W_SSD), per_layer2),
        pl.BlockSpec((None, 1, W_SSD), per_layer2),
    ]
    args += list(params)
    aliases = {}
    if has_stack:
        in_specs.append(pl.BlockSpec(memory_space=pl.ANY))
        args.append(ssm_stack)
        aliases = {len(args) - 1: 2}
    return pl.pallas_call(
        functools.partial(_mixer_kernel, tv=tv, nchunks=nchunks, start_pos=start_pos, has_init=has_init,
                          has_stack=has_stack, layer=layer),
        grid=(nb, nchunks),
        in_specs=in_specs,
        out_specs=[
            pl.BlockSpec((tv, D_INNER), row_map),
            pl.BlockSpec((None, POOL_BUF, W_POOL), per_b),
            (pl.BlockSpec((None, None, W_SSD, STATE), per_layer_b) if has_stack
             else pl.BlockSpec((DEPTH, None, W_SSD, STATE), lambda b, c: (0, b, 0, 0))),
        ],
        out_shape=[
            jax.ShapeDtypeStruct((nb * seq, D_INNER), BF16),
            jax.ShapeDtypeStruct((nb, POOL_BUF, W_POOL), F32),
            jax.ShapeDtypeStruct((DEPTH, nb, W_SSD, STATE), F32),
        ],
        input_output_aliases=aliases,
        scratch_shapes=[
            pltpu.VMEM((POOL_BUF + tv, W_POOL), F32),
            pltpu.VMEM((2, POOL_BUF + tv, W_POOL), F32),
            pltpu.VMEM((GROUPS, STATE, GROUP_W), F32),
        ],
        compiler_params=pltpu.CompilerParams(
            dimension_semantics=("parallel", "arbitrary"), vmem_limit_bytes=VMEM_LIMIT),
        name="mixer_init" if has_init else "mixer_zero",
    )(*args)


def _pad_rows_front(a, rows):
    pad = rows - a.shape[-2]
    return jnp.pad(a, [(0, 0)] * (a.ndim - 2) + [(pad, 0), (0, 0)])


def _pad_lanes(a):
    return jnp.pad(a, [(0, 0)] * (a.ndim - 1) + [(0, LANES - a.shape[-1])])


def kernel(x_prompt, x_sample, state_pool, state_conv, state_ssm, c_prompt, c_sample,
           w_ada, b_ada, w_in, w_pool, pool_scale, conv_w, conv_b,
           dt_bias, a_log, d_skip, ssd_norm_w, w_out, ln_g, ln_b):
    nbp, seq_p, _ = x_prompt.shape
    nbs, seq_s, _ = x_sample.shape
    past_len = 1024

    n_c = nbp + nbs
    c_rows = -(-n_c // 8) * 8
    c_all = jnp.concatenate([c_prompt, c_sample, jnp.zeros((c_rows - n_c, D_MODEL), F32)], axis=0)
    mod = _ada(c_all, w_ada, b_ada)

    w_in_t = jnp.swapaxes(w_in, 1, 2)
    w_dt_b = _pad_lanes(w_in[:, :, MAIN_DIM:]).astype(BF16)
    conv_b3 = conv_b[:, None]
    params = (w_pool.astype(BF16), pool_scale[:, None],
              _pad_lanes(dt_bias)[:, None], _pad_lanes(a_log)[:, None],
              jnp.repeat(d_skip, HEAD_DIM, axis=-1)[:, None], ssd_norm_w[:, None])
    conv0 = _pad_rows_front(state_conv, CONV_BUF)
    init = (_pad_rows_front(state_pool, POOL_BUF),
            state_ssm.reshape(DEPTH, nbs, W_SSD, STATE))

    def in_projections(h, l, tm, hist, n_seg, tiles_per_seq, side=None):
        many = h.shape[0] > 1024
        plain = dict(tm=1024, tn=512, ring=True) if many else dict(tm=h.shape[0], tn=1024)
        u = _inproj(h, w_in_t, l, OFF_U, W_POOL, "raw", **plain)
        gz = _inproj(h, w_in_t, l, OFF_G, W_POOL + W_SSD, "silu", **plain)
        xc, nconv, *side_b = _inproj(h, w_in_t, l, OFF_XBC, CONV_DIM, "conv", tm, 512, side=side,
                                     side_tn=OUT_TN, conv=(conv_w, conv_b3, hist), n_seg=n_seg,
                                     tiles_per_seq=tiles_per_seq, ring=many)
        return (u, gz, xc, nconv, *side_b)

    xp = x_prompt.reshape(nbp * seq_p, D_MODEL)
    xs = x_sample.reshape(nbs * seq_s, D_MODEL)
    outs = {k: [] for k in ("pp", "pc", "psm", "sp", "sc", "ssm")}
    tm_p = 1024
    ssm_p = ssm_s = None
    for l in range(DEPTH):
        shift, scale, gate = (mod[l, :, i * D_MODEL:(i + 1) * D_MODEL] for i in range(3))

        sc3, sh3, g3 = (v[0:nbp].reshape(nbp, 1, D_MODEL) for v in (scale, shift, gate))
        h, dt_raw = _lnmod(xp, sc3, sh3, w_dt_b, l, tm=512)
        u, gz, xc, nconv, w_out_b = in_projections(h, l, tm_p, None, 1, seq_p // tm_p, side=w_out)
        mixed, npool, ssm_p = _mixer(u, gz, xc, dt_raw, None, ssm_p, params, l, nbp, seq_p, 0)
        xp = _outproj(mixed, w_out_b, xp, g3, ln_g, ln_b, l, tm=512, tn=OUT_TN)
        outs["pp"].append(npool[:, 1:])
        outs["pc"].append(nconv[:, CONV_BUF - (CONV_W - 1):])

        sc3, sh3, g3 = (v[nbp:n_c].reshape(1, nbs, D_MODEL) for v in (scale, shift, gate))
        h, dt_raw = _lnmod(xs, sc3, sh3, w_dt_b, l, tm=nbs * seq_s)
        u, gz, xc, nconv = in_projections(h, l, nbs * seq_s, conv0, nbs, 1)
        mixed, npool, ssm_s = _mixer(u, gz, xc, dt_raw, init, ssm_s, params, l, nbs, seq_s, past_len)
        xs = _outproj(mixed, w_out_b, xs, g3, ln_g, ln_b, l, tm=nbs * seq_s, tn=OUT_TN)
        outs["sp"].append(npool[:, 1:])
        outs["sc"].append(nconv[:, CONV_BUF - (CONV_W - 1):])

    return (xp.reshape(nbp, seq_p, D_MODEL), xs.reshape(nbs, seq_s, D_MODEL),
            jnp.stack(outs["pp"]), jnp.stack(outs["pc"]), ssm_p.reshape(DEPTH, nbp, HEADS, HEAD_DIM, STATE),
            jnp.stack(outs["sp"]), jnp.stack(outs["sc"]), ssm_s.reshape(DEPTH, nbs, HEADS, HEAD_DIM, STATE))
```

```python
import functools
import math

import jax
import jax.numpy as jnp
from jax import lax
from jax.experimental import pallas as pl
from jax.experimental.pallas import tpu as pltpu

F32 = jnp.float32
BF16 = jnp.bfloat16

D_MODEL = 4096
DEPTH = 2
D_INNER = 2 * D_MODEL
W_POOL = D_INNER // 4
POOL_WINDOWS = (2, 4, 8, 16)
POOL_GW = W_POOL // len(POOL_WINDOWS)
W_SSD = D_INNER - W_POOL
HEAD_DIM = 64
HEADS = W_SSD // HEAD_DIM
GROUPS = 8
HPG = HEADS // GROUPS
GROUP_W = HPG * HEAD_DIM
STATE = 128
CONV_W = 4
CONV_DIM = W_SSD + 2 * GROUPS * STATE
MAIN_DIM = 2 * W_POOL + W_SSD + CONV_DIM
ALPHA = (2 * DEPTH) ** 0.25
LN_EPS = 1e-5
RMS_EPS = 1e-5
LOG2E = math.log2(math.e)

LANES = 128
POOL_BUF = 16
CONV_BUF = 8
MIX_T = 128
PAIRS_PER_GROUP = GROUP_W // LANES
assert MIX_T == STATE == LANES and 2 * HEAD_DIM == LANES
VMEM_LIMIT = 56 * 1024 * 1024
RING_SLOTS = 3

ROW_TILE = 512
OUT_TN = 256
IN_TILE_MANY = (1024, 512)
IN_TILE_ONE = 1024
CONV_TN = 512

OFF_U, OFF_G, OFF_XBC = 0, W_POOL, 2 * W_POOL + W_SSD


def _silu_of_double(h):
    return h * jnp.tanh(h) + h


def _silu(x):
    return _silu_of_double(0.5 * x)


def _layer_norm(x):
    mu = jnp.mean(x, axis=-1, keepdims=True)
    xc = x - mu
    var = jnp.mean(xc * xc, axis=-1, keepdims=True)
    return xc * lax.rsqrt(var + LN_EPS)


def _ada_kernel(c_ref, w_ref, b_ref, o_ref):
    sc = _silu(c_ref[...]).astype(BF16)
    o_ref[...] = jnp.dot(sc, w_ref[...].astype(BF16), preferred_element_type=F32) + b_ref[...]


def _ada(c_all, w_ada, b_ada, tn=512):
    rows = c_all.shape[0]
    n = w_ada.shape[-1]
    return pl.pallas_call(
        _ada_kernel,
        grid=(DEPTH, n // tn),
        in_specs=[
            pl.BlockSpec((rows, D_MODEL), lambda l, j: (0, 0)),
            pl.BlockSpec((None, D_MODEL, tn), lambda l, j: (l, 0, j)),
            pl.BlockSpec((None, 1, tn), lambda l, j: (l, 0, j)),
        ],
        out_specs=pl.BlockSpec((None, rows, tn), lambda l, j: (l, 0, j)),
        out_shape=jax.ShapeDtypeStruct((DEPTH, rows, n), F32),
        compiler_params=pltpu.CompilerParams(
            dimension_semantics=("arbitrary", "arbitrary"), vmem_limit_bytes=VMEM_LIMIT),
        name="ada_mod",
    )(c_all, w_ada, b_ada.reshape(DEPTH, 1, n))


def _lnmod_kernel(x_ref, sc_ref, sh_ref, wdt_ref, h_ref, dt_ref, *, sub):
    n_mod = sc_ref.shape[0]
    rows_per_mod = x_ref.shape[0] // n_mod
    step = min(sub, rows_per_mod)
    for r in range(n_mod):
        for s0 in range(0, rows_per_mod, step):
            rs = slice(r * rows_per_mod + s0, r * rows_per_mod + s0 + step)
            h = _layer_norm(x_ref[rs, :]) * (1.0 + sc_ref[r:r + 1, :]) + sh_ref[r:r + 1, :]
            h_ref[rs, :] = h.astype(BF16)
    dt_ref[...] = jnp.dot(h_ref[...], wdt_ref[...], preferred_element_type=F32)


def _lnmod(x2d, scale3, shift3, w_dt_b, layer, tm):
    m = x2d.shape[0]
    n_mod = scale3.shape[1]
    tiles_per_mod_block = (m // tm) // scale3.shape[0]
    mod_spec = pl.BlockSpec((None, n_mod, D_MODEL), lambda i: (i // tiles_per_mod_block, 0, 0))
    return pl.pallas_call(
        functools.partial(_lnmod_kernel, sub=128),
        grid=(m // tm,),
        in_specs=[
            pl.BlockSpec((tm, D_MODEL), lambda i: (i, 0)),
            mod_spec,
            mod_spec,
            pl.BlockSpec((None, D_MODEL, LANES), lambda i: (layer, 0, 0)),
        ],
        out_specs=[
            pl.BlockSpec((tm, D_MODEL), lambda i: (i, 0)),
            pl.BlockSpec((tm, LANES), lambda i: (i, 0)),
        ],
        out_shape=[
            jax.ShapeDtypeStruct((m, D_MODEL), BF16),
            jax.ShapeDtypeStruct((m, LANES), F32),
        ],
        compiler_params=pltpu.CompilerParams(
            dimension_semantics=("parallel",), vmem_limit_bytes=VMEM_LIMIT),
        name="ln_mod",
    )(x2d, scale3, shift3, w_dt_b)


def _ring_fetch(src_tile, bufs, sems, step, n_steps):
    slots = bufs.shape[0]

    def copy(k):
        return pltpu.make_async_copy(src_tile(k), bufs.at[k % slots], sems.at[k % slots])

    @pl.when(step == 0)
    def _():
        for k in range(min(slots - 1, n_steps)):
            copy(k).start()

    @pl.when(step + (slots - 1) < n_steps)
    def _():
        copy(step + (slots - 1)).start()

    copy(step).wait()
    return bufs.at[step % slots]


def _inproj_kernel(*refs, mode, n_seg, has_init, tiles_per_seq, sub, side_cast, ring, tm, m_tiles, n_steps):
    refs = list(refs)
    if ring:
        hsem = refs.pop()
        hbuf = refs.pop()
    if side_cast:
        n_in = {"conv": 5 if has_init else 4}.get(mode, 2)
        side_ref = refs.pop(n_in)
        side_b_ref = refs.pop(n_in + (2 if mode == "conv" else 1))
        for jj in range(side_b_ref.shape[0]):
            cw = side_b_ref.shape[2]
            side_b_ref[jj] = side_ref[:, jj * cw:(jj + 1) * cw].astype(BF16)
    if mode == "conv":
        if has_init:
            h_ref, w_ref, cw_ref, cb_ref, hist_ref, o_ref, st_ref, wb_ref, buf = refs
        else:
            h_ref, w_ref, cw_ref, cb_ref, o_ref, st_ref, wb_ref, buf = refs
    else:
        h_ref, w_ref, o_ref, wb_ref = refs
    i = pl.program_id(1)
    if ring:
        h_hbm = h_ref

        def h_rows(k):
            row0 = (k % m_tiles) * tm
            return h_hbm.at[pl.ds(row0 if isinstance(row0, int) else pl.multiple_of(row0, tm), tm)]

        h_ref = _ring_fetch(h_rows, hbuf, hsem, pl.program_id(0) * m_tiles + i, n_steps)

    @pl.when(i == 0)
    def _():
        wb_ref[...] = w_ref[...].astype(BF16)

    def project(r0, n):
        return lax.dot_general(h_ref[r0:r0 + n, :], wb_ref[...], (((1,), (1,)), ((), ())),
                               preferred_element_type=F32)

    if mode == "raw":
        o_ref[...] = project(0, tm)
    elif mode == "silu":
        o_ref[...] = _silu(project(0, tm))
    else:
        seg = tm // n_seg
        half = tm // 2 if (n_seg == 1 and tm >= 2 * sub) else tm
        if not has_init:
            @pl.when(i % tiles_per_seq == 0)
            def _():
                buf[0, 0:CONV_BUF, :] = jnp.zeros((CONV_BUF, buf.shape[-1]), F32)
        for s in range(n_seg):
            if has_init:
                buf[s, 0:CONV_BUF, :] = hist_ref[s]
        cw_half = 0.5 * cw_ref[...]
        cb_half = 0.5 * cb_ref[...]
        for h0 in range(0, tm, half):
            res = project(h0, half)
            for s in range(h0 // seg, max(h0 // seg + 1, (h0 + half) // seg)):
                lo = max(h0, s * seg) - s * seg
                hi = min(h0 + half, (s + 1) * seg) - s * seg
                buf[s, CONV_BUF + lo:CONV_BUF + hi, :] = res[s * seg + lo - h0:s * seg + hi - h0, :]
                for r0 in range(lo, hi, sub):
                    n = min(sub, hi - r0)
                    acc = cb_half + buf[s, CONV_BUF + r0:CONV_BUF + r0 + n, :] * cw_half[CONV_W - 1:CONV_W, :]
                    for back in range(1, CONV_W):
                        acc = acc + (buf[s, CONV_BUF + r0 - back:CONV_BUF + r0 - back + n, :]
                                     * cw_half[CONV_W - 1 - back:CONV_W - back, :])
                    o_ref[s * seg + r0:s * seg + r0 + n, :] = _silu_of_double(acc)
        for s in range(n_seg):
            st_ref[s] = buf[s, seg:seg + CONV_BUF, :]
            if not has_init:
                buf[s, 0:CONV_BUF, :] = buf[s, seg:seg + CONV_BUF, :]


def _inproj(h2d, w_in_t, layer, col0, width, mode, tm, tn, conv=None, n_seg=1, tiles_per_seq=1, side=None,
            side_tn=None, ring=False):
    m = h2d.shape[0]
    j0 = col0 // tn
    m_tiles = m // tm
    in_specs = [
        pl.BlockSpec(memory_space=pl.ANY) if ring else pl.BlockSpec((tm, D_MODEL), lambda j, i: (i, 0)),
        pl.BlockSpec((None, tn, D_MODEL), lambda j, i: (layer, j0 + j, 0)),
    ]
    args = [h2d, w_in_t]
    out_specs = pl.BlockSpec((tm, tn), lambda j, i: (i, j))
    out_shape = jax.ShapeDtypeStruct((m, width), F32)
    scratch = [pltpu.VMEM((tn, D_MODEL), BF16)]
    has_init = False
    if mode == "conv":
        conv_w, conv_b, hist = conv
        has_init = hist is not None
        in_specs += [pl.BlockSpec((None, CONV_W, tn), lambda j, i: (layer, 0, j)),
                     pl.BlockSpec((None, 1, tn), lambda j, i: (layer, 0, j))]
        args += [conv_w, conv_b]
        if has_init:
            in_specs.append(pl.BlockSpec((None, n_seg, CONV_BUF, tn), lambda j, i: (layer, 0, 0, j)))
            args.append(hist)
        n_seq = (m // tm) * n_seg // tiles_per_seq
        out_specs = [out_specs,
                     pl.BlockSpec((n_seg, CONV_BUF, tn), lambda j, i: (i // tiles_per_seq, 0, j))]
        out_shape = [out_shape, jax.ShapeDtypeStruct((n_seq, CONV_BUF, width), F32)]
        scratch.append(pltpu.VMEM((n_seg, CONV_BUF + tm // n_seg, tn), F32))
    if side is not None:
        _, side_rows, side_cols = side.shape
        slab = side_rows // ((width // tn) * m_tiles)
        in_specs.append(pl.BlockSpec((None, slab, side_cols), lambda j, i: (layer, j * m_tiles + i, 0)))
        args.append(side)
        n_side_tiles = side_cols // side_tn
        out_specs = (out_specs if isinstance(out_specs, list) else [out_specs]) + [
            pl.BlockSpec((n_side_tiles, slab, side_tn), lambda j, i: (0, j * m_tiles + i, 0))]
        out_shape = (out_shape if isinstance(out_shape, list) else [out_shape]) + [
            jax.ShapeDtypeStruct((n_side_tiles, side_rows, side_tn), BF16)]
    n_steps = (width // tn) * m_tiles
    if ring:
        scratch += [pltpu.VMEM((RING_SLOTS, tm, D_MODEL), BF16), pltpu.SemaphoreType.DMA((RING_SLOTS,))]
    return pl.pallas_call(
        functools.partial(_inproj_kernel, mode=mode, n_seg=n_seg, has_init=has_init,
                          tiles_per_seq=tiles_per_seq, sub=256, side_cast=side is not None,
                          ring=ring, tm=tm, m_tiles=m_tiles, n_steps=n_steps),
        grid=(width // tn, m // tm),
        in_specs=in_specs,
        out_specs=out_specs,
        out_shape=out_shape,
        scratch_shapes=scratch,
        compiler_params=pltpu.CompilerParams(
            dimension_semantics=("arbitrary", "arbitrary"), vmem_limit_bytes=VMEM_LIMIT),
        name="in_proj_" + mode,
    )(*args)


def _outproj_kernel(m_ref, w_hbm, x_ref, g_ref, lng_ref, lnb_ref, o_ref, acc_ref, wbuf, wsem, *, nj, tn, sub,
                    n_steps):
    j = pl.program_id(1)
    n_mod = g_ref.shape[0]
    tm = m_ref.shape[0]
    rows_per_mod = tm // n_mod
    w_ref = _ring_fetch(lambda k: w_hbm.at[k % nj], wbuf, wsem, pl.program_id(0) * nj + j, n_steps)
    o = jnp.dot(m_ref[...], w_ref[...], preferred_element_type=F32)
    for r in range(n_mod):
        rs = slice(r * rows_per_mod, (r + 1) * rows_per_mod)
        acc_ref[j, rs, :] = ALPHA * x_ref[rs, :] + g_ref[r:r + 1, :] * o[rs, :]

    @pl.when(j == nj - 1)
    def _():
        inv_d = 1.0 / (nj * tn)
        for s0 in range(0, tm, sub):
            rs = slice(s0, s0 + sub)
            total = acc_ref[0, rs, :].sum(axis=-1, keepdims=True)
            for jj in range(1, nj):
                total = total + acc_ref[jj, rs, :].sum(axis=-1, keepdims=True)
            mu = total * inv_d
            sq = jnp.zeros_like(mu)
            for jj in range(nj):
                d = acc_ref[jj, rs, :] - mu
                sq = sq + (d * d).sum(axis=-1, keepdims=True)
            inv = lax.rsqrt(sq * inv_d + LN_EPS)
            for jj in range(nj):
                cs = slice(jj * tn, (jj + 1) * tn)
                o_ref[rs, cs] = (acc_ref[jj, rs, :] - mu) * inv * lng_ref[:, cs] + lnb_ref[:, cs]


def _outproj(mixed, w_out_b, x2d, gate3, ln_g, ln_b, layer, tm, tn):
    m = x2d.shape[0]
    n_mod = gate3.shape[1]
    tiles_per_mod_block = (m // tm) // gate3.shape[0]
    nj = D_MODEL // tn
    vec_spec = pl.BlockSpec((None, 1, D_MODEL), lambda i, j: (layer, 0, 0))
    return pl.pallas_call(
        functools.partial(_outproj_kernel, nj=nj, tn=tn, sub=min(128, tm), n_steps=(m // tm) * nj),
        grid=(m // tm, nj),
        in_specs=[
            pl.BlockSpec((tm, D_INNER), lambda i, j: (i, 0)),
            pl.BlockSpec(memory_space=pl.ANY),
            pl.BlockSpec((tm, tn), lambda i, j: (i, j)),
            pl.BlockSpec((None, n_mod, tn), lambda i, j: (i // tiles_per_mod_block, 0, j)),
            vec_spec,
            vec_spec,
        ],
        out_specs=pl.BlockSpec((tm, D_MODEL), lambda i, j: (i, 0)),
        out_shape=jax.ShapeDtypeStruct((m, D_MODEL), F32),
        scratch_shapes=[pltpu.VMEM((nj, tm, tn), F32), pltpu.VMEM((RING_SLOTS, D_INNER, tn), BF16),
                        pltpu.SemaphoreType.DMA((RING_SLOTS,))],
        compiler_params=pltpu.CompilerParams(
            dimension_semantics=("arbitrary", "arbitrary"), vmem_limit_bytes=VMEM_LIMIT),
        name="out_proj",
    )(mixed, w_out_b, x2d, gate3, ln_g.reshape(DEPTH, 1, D_MODEL), ln_b.reshape(DEPTH, 1, D_MODEL))


def _split3(a):
    hi = a.astype(BF16)
    r1 = a - hi.astype(F32)
    mid = r1.astype(BF16)
    lo = (r1 - mid.astype(F32)).astype(BF16)
    return hi, mid, lo


def _mixer_kernel(*refs, tv, nchunks, start_pos, has_init, has_stack, layer):
    if has_init:
        (u_ref, gz_ref, xc_ref, dtr_ref, pool0_ref, ssm0_ref, *rest) = refs
    else:
        (u_ref, gz_ref, xc_ref, dtr_ref, *rest) = refs
    (wpool_ref, pscale_ref, dtb_ref, alog_ref, dskip_ref, normw_ref, *rest) = rest
    if has_stack:
        rest = rest[1:]
    mixed_ref, npool_ref, nssm_ref, poolbuf, winbuf, h_ref = rest
    t = MIX_T
    c = pl.program_id(1)

    def pad_rows(v):
        if tv == t:
            return v
        return jnp.concatenate([v, jnp.zeros((t - tv, v.shape[1]), v.dtype)], axis=0)

    @pl.when(c == 0)
    def _init():
        if has_init:
            poolbuf[0:POOL_BUF, :] = pool0_ref[...]
            for g in range(GROUPS):
                for j in range(PAIRS_PER_GROUP):
                    r0 = (g * HPG + 2 * j) * HEAD_DIM
                    h_ref[g, :, j * LANES:(j + 1) * LANES] = ssm0_ref[r0:r0 + LANES, :].T
        else:
            poolbuf[0:POOL_BUF, :] = jnp.zeros((POOL_BUF, W_POOL), F32)
            h_ref[...] = jnp.zeros(h_ref.shape, F32)

    poolbuf[POOL_BUF:POOL_BUF + tv, :] = u_ref[...]
    row = lax.broadcasted_iota(jnp.int32, (tv, 1), 0)
    pos = start_pos + c * tv + row
    for g, w in enumerate(POOL_WINDOWS):
        cols = slice(g * POOL_GW, (g + 1) * POOL_GW)
        u_g = poolbuf[POOL_BUF:POOL_BUF + tv, cols]
        src, d, level = poolbuf, 1, 0
        while 2 * d < w:
            lo = POOL_BUF - (w - 2 * d)
            dst = winbuf.at[level % 2]
            dst[lo:POOL_BUF + tv, cols] = src[lo:POOL_BUF + tv, cols] + src[lo - d:POOL_BUF + tv - d, cols]
            src, d, level = dst, 2 * d, level + 1
        s = src[POOL_BUF:POOL_BUF + tv, cols] + src[POOL_BUF - d:POOL_BUF + tv - d, cols]
        cnt = jnp.minimum(pos + 1, w).astype(F32)
        pooled = (s / cnt - u_g).astype(BF16)
        po = jnp.dot(pooled, wpool_ref[g], preferred_element_type=F32)
        mixed_ref[:, cols] = (po * pscale_ref[:, cols] * gz_ref[:, cols]).astype(BF16)
    poolbuf[0:POOL_BUF, :] = poolbuf[tv:tv + POOL_BUF, :]

    x_dt = dtr_ref[...] + dtb_ref[...]
    dt = jnp.maximum(x_dt, 0.0) + jnp.log1p(jnp.exp(-jnp.abs(x_dt)))
    if tv < t:
        dt = jnp.concatenate([dt, jnp.zeros((t - tv, LANES), F32)], axis=0)
    a = dt * (-LOG2E * jnp.exp(alog_ref[...]))
    ri = lax.broadcasted_iota(jnp.int32, (t, t), 0)
    ci = lax.broadcasted_iota(jnp.int32, (t, t), 1)
    tril = ri >= ci
    tri = jnp.where(tril, 1.0, 0.0).astype(BF16)
    a_cs = sum(jnp.dot(tri, piece, preferred_element_type=F32) for piece in _split3(a))
    a_last = a_cs[t - 1:t, :]
    ea_last = jnp.exp2(a_last)
    a_cs_t = a_cs.T
    dt_t = dt.T
    w_t = (jnp.exp2(a_last - a_cs) * dt).T
    lo_half = lax.broadcasted_iota(jnp.int32, (1, LANES), 1) < HEAD_DIM

    def block_diag(v):
        zero = jnp.zeros_like(v)
        return jnp.concatenate([jnp.where(lo_half, v, zero), jnp.where(lo_half, zero, v)], axis=0)

    for g in range(GROUPS):
        b_g = pad_rows(xc_ref[:, W_SSD + g * STATE:W_SSD + (g + 1) * STATE])
        c_g = pad_rows(xc_ref[:, W_SSD + GROUPS * STATE + g * STATE:W_SSD + GROUPS * STATE + (g + 1) * STATE])
        cb = lax.dot_general(c_g.astype(BF16), b_g.astype(BF16), (((1,), (1,)), ((), ())),
                             preferred_element_type=F32)
        b_t = b_g.T
        ys = []
        for j in range(PAIRS_PER_GROUP):
            p = g * PAIRS_PER_GROUP + j
            pc = slice(p * LANES, (p + 1) * LANES)
            jc = slice(j * LANES, (j + 1) * LANES)
            xs_p = pad_rows(xc_ref[:, pc])
            h_p = h_ref[g, :, jc]
            bd_x = block_diag(xs_p.astype(BF16))
            bd_h = block_diag(h_p.astype(BF16))
            m_parts, e_parts, b_parts = [], [], []
            for k in (2 * p, 2 * p + 1):
                col = jnp.broadcast_to(a_cs[0:tv, k:k + 1], (tv, t))
                lmat = jnp.where(tril[0:tv], jnp.exp2(col - a_cs_t[k:k + 1, :]), 0.0)
                m_parts.append((cb[0:tv] * lmat * dt_t[k:k + 1, :]).astype(BF16))
                e_parts.append((c_g[0:tv] * jnp.exp2(col)).astype(BF16))
                b_parts.append((b_t * w_t[k:k + 1, :]).astype(BF16))
            y_mm = jnp.dot(jnp.concatenate(m_parts + e_parts, axis=1),
                           jnp.concatenate([bd_x, bd_h], axis=0), preferred_element_type=F32)
            ys.append(y_mm + dskip_ref[:, pc] * xs_p[0:tv])
            st = jnp.dot(jnp.concatenate(b_parts, axis=1), bd_x, preferred_element_type=F32)
            cd = jnp.where(lo_half, ea_last[:, 2 * p:2 * p + 1], ea_last[:, 2 * p + 1:2 * p + 2])
            h_ref[g, :, jc] = cd * h_p + st
        gc = slice(g * GROUP_W, (g + 1) * GROUP_W)
        y_g = jnp.concatenate(ys, axis=1)
        y_g = y_g * gz_ref[:, W_POOL + g * GROUP_W:W_POOL + (g + 1) * GROUP_W]
        ms = jnp.mean(y_g * y_g, axis=-1, keepdims=True)
        out = y_g * lax.rsqrt(ms + RMS_EPS) * normw_ref[:, gc]
        mixed_ref[:, W_POOL + g * GROUP_W:W_POOL + (g + 1) * GROUP_W] = out.astype(BF16)

    @pl.when(c == nchunks - 1)
    def _fin():
        npool_ref[...] = poolbuf[0:POOL_BUF, :]
        if has_stack:
            state_out = nssm_ref
        else:
            state_out = nssm_ref.at[layer]
            for other in range(DEPTH):
                if other != layer:
                    nssm_ref[other] = jnp.zeros((W_SSD, STATE), F32)
        for g in range(GROUPS):
            for j in range(PAIRS_PER_GROUP):
                r0 = (g * HPG + 2 * j) * HEAD_DIM
                state_out[r0:r0 + LANES, :] = h_ref[g, :, j * LANES:(j + 1) * LANES].T


def _mixer(u, gz, xc, dt_raw, init, ssm_stack, params, layer, nb, seq, start_pos):
    tv = min(seq, MIX_T)
    nchunks = seq // tv
    has_init = init is not None
    has_stack = ssm_stack is not None
    row_map = lambda b, c: (b * nchunks + c, 0)
    per_layer2 = lambda b, c: (layer, 0, 0)
    per_b = lambda b, c: (b, 0, 0)
    per_layer_b = lambda b, c: (layer, b, 0, 0)
    in_specs = [pl.BlockSpec((tv, W_POOL), row_map), pl.BlockSpec((tv, W_POOL + W_SSD), row_map),
                pl.BlockSpec((tv, CONV_DIM), row_map), pl.BlockSpec((tv, LANES), row_map)]
    args = [u, gz, xc, dt_raw]
    if has_init:
        in_specs += [pl.BlockSpec((None, None, POOL_BUF, W_POOL), per_layer_b),
                     pl.BlockSpec((None, None, W_SSD, STATE), per_layer_b)]
        args += list(init)
    in_specs += [
        pl.BlockSpec((None, len(POOL_WINDOWS), POOL_GW, POOL_GW), lambda b, c: (layer, 0, 0, 0)),
        pl.BlockSpec((None, 1, W_POOL), per_layer2),
        pl.BlockSpec((None, 1, LANES), per_layer2),
        pl.BlockSpec((None, 1, LANES), per_layer2),
        pl.BlockSpec((None, 1, W_SSD), per_layer2),
        pl.BlockSpec((None, 1, W_SSD), per_layer2),
    ]
    args += list(params)
    aliases = {}
    if has_stack:
        in_specs.append(pl.BlockSpec(memory_space=pl.ANY))
        args.append(ssm_stack)
        aliases = {len(args) - 1: 2}
    return pl.pallas_call(
        functools.partial(_mixer_kernel, tv=tv, nchunks=nchunks, start_pos=start_pos, has_init=has_init,
                          has_stack=has_stack, layer=layer),
        grid=(nb, nchunks),
        in_specs=in_specs,
        out_specs=[
            pl.BlockSpec((tv, D_INNER), row_map),
            pl.BlockSpec((None, POOL_BUF, W_POOL), per_b),
            (pl.BlockSpec((None, None, W_SSD, STATE), per_layer_b) if has_stack
             else pl.BlockSpec((DEPTH, None, W_SSD, STATE), lambda b, c: (0, b, 0, 0))),
        ],
        out_shape=[
            jax.ShapeDtypeStruct((nb * seq, D_INNER), BF16),
            jax.ShapeDtypeStruct((nb, POOL_BUF, W_POOL), F32),
            jax.ShapeDtypeStruct((DEPTH, nb, W_SSD, STATE), F32),
        ],
        input_output_aliases=aliases,
        scratch_shapes=[
            pltpu.VMEM((POOL_BUF + tv, W_POOL), F32),
            pltpu.VMEM((2, POOL_BUF + tv, W_POOL), F32),
            pltpu.VMEM((GROUPS, STATE, GROUP_W), F32),
        ],
        compiler_params=pltpu.CompilerParams(
            dimension_semantics=("parallel", "arbitrary"), vmem_limit_bytes=VMEM_LIMIT),
        name="mixer_init" if has_init else "mixer_zero",
    )(*args)


def _pad_rows_front(a, rows):
    pad = rows - a.shape[-2]
    return jnp.pad(a, [(0, 0)] * (a.ndim - 2) + [(pad, 0), (0, 0)])


def _pad_lanes(a):
    return jnp.pad(a, [(0, 0)] * (a.ndim - 1) + [(0, LANES - a.shape[-1])])


def kernel(x_prompt, x_sample, state_pool, state_conv, state_ssm, c_prompt, c_sample,
           w_ada, b_ada, w_in, w_pool, pool_scale, conv_w, conv_b,
           dt_bias, a_log, d_skip, ssd_norm_w, w_out, ln_g, ln_b):
    nbp, seq_p, _ = x_prompt.shape
    nbs, seq_s, _ = x_sample.shape
    past_len = 1024

    n_c = nbp + nbs
    c_rows = -(-n_c // 8) * 8
    c_all = jnp.concatenate([c_prompt, c_sample, jnp.zeros((c_rows - n_c, D_MODEL), F32)], axis=0)
    mod = _ada(c_all, w_ada, b_ada)

    w_in_t = jnp.swapaxes(w_in, 1, 2)
    w_dt_b = _pad_lanes(w_in[:, :, MAIN_DIM:]).astype(BF16)
    conv_b3 = conv_b[:, None]
    params = (w_pool.astype(BF16), pool_scale[:, None],
              _pad_lanes(dt_bias)[:, None], _pad_lanes(a_log)[:, None],
              jnp.repeat(d_skip, HEAD_DIM, axis=-1)[:, None], ssd_norm_w[:, None])
    conv0 = _pad_rows_front(state_conv, CONV_BUF)
    init = (_pad_rows_front(state_pool, POOL_BUF),
            state_ssm.reshape(DEPTH, nbs, W_SSD, STATE))

    def in_projections(h, l, tm, hist, n_seg, tiles_per_seq, side=None):
        many = h.shape[0] > tm
        plain = (dict(tm=IN_TILE_MANY[0], tn=IN_TILE_MANY[1], ring=True) if many
                 else dict(tm=h.shape[0], tn=IN_TILE_ONE))
        u = _inproj(h, w_in_t, l, OFF_U, W_POOL, "raw", **plain)
        gz = _inproj(h, w_in_t, l, OFF_G, W_POOL + W_SSD, "silu", **plain)
        xc, nconv, *side_b = _inproj(h, w_in_t, l, OFF_XBC, CONV_DIM, "conv", tm, CONV_TN, side=side,
                                     side_tn=OUT_TN, conv=(conv_w, conv_b3, hist), n_seg=n_seg,
                                     tiles_per_seq=tiles_per_seq, ring=many)
        return (u, gz, xc, nconv, *side_b)

    xp = x_prompt.reshape(nbp * seq_p, D_MODEL)
    xs = x_sample.reshape(nbs * seq_s, D_MODEL)
    outs = {k: [] for k in ("pp", "pc", "sp", "sc")}
    tm_p = IN_TILE_MANY[0]
    tm_s = nbs * seq_s
    ssm_p = ssm_s = None
    for l in range(DEPTH):
        shift, scale, gate = (mod[l, :, i * D_MODEL:(i + 1) * D_MODEL] for i in range(3))

        sc3, sh3, g3 = (v[0:nbp].reshape(nbp, 1, D_MODEL) for v in (scale, shift, gate))
        h, dt_raw = _lnmod(xp, sc3, sh3, w_dt_b, l, tm=ROW_TILE)
        u, gz, xc, nconv, w_out_b = in_projections(h, l, tm_p, None, 1, seq_p // tm_p, side=w_out)
        mixed, npool, ssm_p = _mixer(u, gz, xc, dt_raw, None, ssm_p, params, l, nbp, seq_p, 0)
        xp = _outproj(mixed, w_out_b, xp, g3, ln_g, ln_b, l, tm=ROW_TILE, tn=OUT_TN)
        outs["pp"].append(npool[:, 1:])
        outs["pc"].append(nconv[:, CONV_BUF - (CONV_W - 1):])

        sc3, sh3, g3 = (v[nbp:n_c].reshape(1, nbs, D_MODEL) for v in (scale, shift, gate))
        h, dt_raw = _lnmod(xs, sc3, sh3, w_dt_b, l, tm=tm_s)
        u, gz, xc, nconv = in_projections(h, l, tm_s, conv0, nbs, 1)
        mixed, npool, ssm_s = _mixer(u, gz, xc, dt_raw, init, ssm_s, params, l, nbs, seq_s, past_len)
        xs = _outproj(mixed, w_out_b, xs, g3, ln_g, ln_b, l, tm=tm_s, tn=OUT_TN)
        outs["sp"].append(npool[:, 1:])
        outs["sc"].append(nconv[:, CONV_BUF - (CONV_W - 1):])

    return (xp.reshape(nbp, seq_p, D_MODEL), xs.reshape(nbs, seq_s, D_MODEL),
            jnp.stack(outs["pp"]), jnp.stack(outs["pc"]), ssm_p.reshape(DEPTH, nbp, HEADS, HEAD_DIM, STATE),
            jnp.stack(outs["sp"]), jnp.stack(outs["sc"]), ssm_s.reshape(DEPTH, nbs, HEADS, HEAD_DIM, STATE))
```

```python
import functools
import math

import jax
import jax.numpy as jnp
from jax import lax
from jax.experimental import pallas as pl
from jax.experimental.pallas import tpu as pltpu

F32 = jnp.float32
BF16 = jnp.bfloat16

D_MODEL = 4096
DEPTH = 2
D_INNER = 2 * D_MODEL
W_POOL = D_INNER // 4
POOL_WINDOWS = (2, 4, 8, 16)
POOL_GW = W_POOL // len(POOL_WINDOWS)
W_SSD = D_INNER - W_POOL
HEAD_DIM = 64
HEADS = W_SSD // HEAD_DIM
GROUPS = 8
HPG = HEADS // GROUPS
GROUP_W = HPG * HEAD_DIM
STATE = 128
CONV_W = 4
CONV_DIM = W_SSD + 2 * GROUPS * STATE
MAIN_DIM = 2 * W_POOL + W_SSD + CONV_DIM
ALPHA = (2 * DEPTH) ** 0.25
LN_EPS = 1e-5
RMS_EPS = 1e-5
LOG2E = math.log2(math.e)

LANES = 128
POOL_BUF = 16
CONV_BUF = 8
MIX_T = 128
PAIRS_PER_GROUP = GROUP_W // LANES
assert MIX_T == STATE == LANES and 2 * HEAD_DIM == LANES
VMEM_LIMIT = 56 * 1024 * 1024
RING_SLOTS = 3

ROW_TILE = 512
OUT_TN = 256
IN_TILE_MANY = (1024, 512)
IN_TILE_ONE = 1024
CONV_TN = 512

OFF_U, OFF_G, OFF_XBC = 0, W_POOL, 2 * W_POOL + W_SSD


def _silu_of_double(h):
    return h * jnp.tanh(h) + h


def _silu(x):
    return _silu_of_double(0.5 * x)


def _layer_norm(x):
    mu = jnp.mean(x, axis=-1, keepdims=True)
    xc = x - mu
    var = jnp.mean(xc * xc, axis=-1, keepdims=True)
    return xc * lax.rsqrt(var + LN_EPS)


def _ada_kernel(c_ref, w_ref, b_ref, o_ref):
    sc = _silu(c_ref[...]).astype(BF16)
    o_ref[...] = jnp.dot(sc, w_ref[...].astype(BF16), preferred_element_type=F32) + b_ref[...]


def _ada(c_all, w_ada, b_ada, tn=512):
    rows = c_all.shape[0]
    n = w_ada.shape[-1]
    return pl.pallas_call(
        _ada_kernel,
        grid=(DEPTH, n // tn),
        in_specs=[
            pl.BlockSpec((rows, D_MODEL), lambda l, j: (0, 0)),
            pl.BlockSpec((None, D_MODEL, tn), lambda l, j: (l, 0, j)),
            pl.BlockSpec((None, 1, tn), lambda l, j: (l, 0, j)),
        ],
        out_specs=pl.BlockSpec((None, rows, tn), lambda l, j: (l, 0, j)),
        out_shape=jax.ShapeDtypeStruct((DEPTH, rows, n), F32),
        compiler_params=pltpu.CompilerParams(
            dimension_semantics=("arbitrary", "arbitrary"), vmem_limit_bytes=VMEM_LIMIT),
        name="ada_mod",
    )(c_all, w_ada, b_ada.reshape(DEPTH, 1, n))


def _lnmod_kernel(x_ref, sc_ref, sh_ref, wdt_ref, h_ref, dt_ref, *, sub):
    n_mod = sc_ref.shape[0]
    rows_per_mod = x_ref.shape[0] // n_mod
    step = min(sub, rows_per_mod)
    for r in range(n_mod):
        for s0 in range(0, rows_per_mod, step):
            rs = slice(r * rows_per_mod + s0, r * rows_per_mod + s0 + step)
            h = _layer_norm(x_ref[rs, :]) * (1.0 + sc_ref[r:r + 1, :]) + sh_ref[r:r + 1, :]
            h_ref[rs, :] = h.astype(BF16)
    dt_ref[...] = jnp.dot(h_ref[...], wdt_ref[...], preferred_element_type=F32)


def _lnmod(x2d, scale3, shift3, w_dt_b, layer, tm):
    m = x2d.shape[0]
    n_mod = scale3.shape[1]
    tiles_per_mod_block = (m // tm) // scale3.shape[0]
    mod_spec = pl.BlockSpec((None, n_mod, D_MODEL), lambda i: (i // tiles_per_mod_block, 0, 0))
    return pl.pallas_call(
        functools.partial(_lnmod_kernel, sub=128),
        grid=(m // tm,),
        in_specs=[
            pl.BlockSpec((tm, D_MODEL), lambda i: (i, 0)),
            mod_spec,
            mod_spec,
            pl.BlockSpec((None, D_MODEL, LANES), lambda i: (layer, 0, 0)),
        ],
        out_specs=[
            pl.BlockSpec((tm, D_MODEL), lambda i: (i, 0)),
            pl.BlockSpec((tm, LANES), lambda i: (i, 0)),
        ],
        out_shape=[
            jax.ShapeDtypeStruct((m, D_MODEL), BF16),
            jax.ShapeDtypeStruct((m, LANES), F32),
        ],
        compiler_params=pltpu.CompilerParams(
            dimension_semantics=("parallel",), vmem_limit_bytes=VMEM_LIMIT),
        name="ln_mod",
    )(x2d, scale3, shift3, w_dt_b)


def _ring_fetch(src_tile, bufs, sems, step, n_steps):
    slots = bufs.shape[0]

    def copy(k):
        return pltpu.make_async_copy(src_tile(k), bufs.at[k % slots], sems.at[k % slots])

    @pl.when(step == 0)
    def _():
        for k in range(min(slots - 1, n_steps)):
            copy(k).start()

    @pl.when(step + (slots - 1) < n_steps)
    def _():
        copy(step + (slots - 1)).start()

    copy(step).wait()
    return bufs.at[step % slots]


def _inproj_kernel(*refs, mode, n_seg, has_init, tiles_per_seq, sub, side_cast, ring, tm, m_tiles, n_steps):
    refs = list(refs)
    if ring:
        hsem = refs.pop()
        hbuf = refs.pop()
    if side_cast:
        n_in = {"conv": 5 if has_init else 4}.get(mode, 2)
        side_ref = refs.pop(n_in)
        side_b_ref = refs.pop(n_in + (2 if mode == "conv" else 1))
        for jj in range(side_b_ref.shape[0]):
            cw = side_b_ref.shape[2]
            side_b_ref[jj] = side_ref[:, jj * cw:(jj + 1) * cw].astype(BF16)
    if mode == "conv":
        if has_init:
            h_ref, w_ref, cw_ref, cb_ref, hist_ref, o_ref, st_ref, wb_ref, buf = refs
        else:
            h_ref, w_ref, cw_ref, cb_ref, o_ref, st_ref, wb_ref, buf = refs
    else:
        h_ref, w_ref, o_ref, wb_ref = refs
    i = pl.program_id(1)
    if ring:
        h_hbm = h_ref

        def h_rows(k):
            row0 = (k % m_tiles) * tm
            return h_hbm.at[pl.ds(row0 if isinstance(row0, int) else pl.multiple_of(row0, tm), tm)]

        h_ref = _ring_fetch(h_rows, hbuf, hsem, pl.program_id(0) * m_tiles + i, n_steps)

    @pl.when(i == 0)
    def _():
        wb_ref[...] = w_ref[...].astype(BF16)

    def project(r0, n):
        return lax.dot_general(h_ref[r0:r0 + n, :], wb_ref[...], (((1,), (1,)), ((), ())),
                               preferred_element_type=F32)

    if mode == "raw":
        o_ref[...] = project(0, tm)
    elif mode == "silu":
        o_ref[...] = _silu(project(0, tm))
    else:
        seg = tm // n_seg
        half = tm // 2 if (n_seg == 1 and tm >= 2 * sub) else tm
        if not has_init:
            @pl.when(i % tiles_per_seq == 0)
            def _():
                buf[0, 0:CONV_BUF, :] = jnp.zeros((CONV_BUF, buf.shape[-1]), F32)
        for s in range(n_seg):
            if has_init:
                buf[s, 0:CONV_BUF, :] = hist_ref[s]
        cw_half = 0.5 * cw_ref[...]
        cb_half = 0.5 * cb_ref[...]
        for h0 in range(0, tm, half):
            res = project(h0, half)
            for s in range(h0 // seg, max(h0 // seg + 1, (h0 + half) // seg)):
                lo = max(h0, s * seg) - s * seg
                hi = min(h0 + half, (s + 1) * seg) - s * seg
                buf[s, CONV_BUF + lo:CONV_BUF + hi, :] = res[s * seg + lo - h0:s * seg + hi - h0, :]
                for r0 in range(lo, hi, sub):
                    n = min(sub, hi - r0)
                    acc = cb_half + buf[s, CONV_BUF + r0:CONV_BUF + r0 + n, :] * cw_half[CONV_W - 1:CONV_W, :]
                    for back in range(1, CONV_W):
                        acc = acc + (buf[s, CONV_BUF + r0 - back:CONV_BUF + r0 - back + n, :]
                                     * cw_half[CONV_W - 1 - back:CONV_W - back, :])
                    o_ref[s * seg + r0:s * seg + r0 + n, :] = _silu_of_double(acc)
        for s in range(n_seg):
            st_ref[s] = buf[s, seg:seg + CONV_BUF, :]
            if not has_init:
                buf[s, 0:CONV_BUF, :] = buf[s, seg:seg + CONV_BUF, :]


def _inproj(h2d, w_in_t, layer, col0, width, mode, tm, tn, conv=None, n_seg=1, tiles_per_seq=1, side=None,
            side_tn=None, ring=False):
    m = h2d.shape[0]
    j0 = col0 // tn
    m_tiles = m // tm
    in_specs = [
        pl.BlockSpec(memory_space=pl.ANY) if ring else pl.BlockSpec((tm, D_MODEL), lambda j, i: (i, 0)),
        pl.BlockSpec((None, tn, D_MODEL), lambda j, i: (layer, j0 + j, 0)),
    ]
    args = [h2d, w_in_t]
    out_specs = pl.BlockSpec((tm, tn), lambda j, i: (i, j))
    out_shape = jax.ShapeDtypeStruct((m, width), F32)
    scratch = [pltpu.VMEM((tn, D_MODEL), BF16)]
    has_init = False
    if mode == "conv":
        conv_w, conv_b, hist = conv
        has_init = hist is not None
        in_specs += [pl.BlockSpec((None, CONV_W, tn), lambda j, i: (layer, 0, j)),
                     pl.BlockSpec((None, 1, tn), lambda j, i: (layer, 0, j))]
        args += [conv_w, conv_b]
        if has_init:
            in_specs.append(pl.BlockSpec((None, n_seg, CONV_BUF, tn), lambda j, i: (layer, 0, 0, j)))
            args.append(hist)
        n_seq = (m // tm) * n_seg // tiles_per_seq
        out_specs = [out_specs,
                     pl.BlockSpec((n_seg, CONV_BUF, tn), lambda j, i: (i // tiles_per_seq, 0, j))]
        out_shape = [out_shape, jax.ShapeDtypeStruct((n_seq, CONV_BUF, width), F32)]
        scratch.append(pltpu.VMEM((n_seg, CONV_BUF + tm // n_seg, tn), F32))
    if side is not None:
        _, side_rows, side_cols = side.shape
        slab = side_rows // ((width // tn) * m_tiles)
        in_specs.append(pl.BlockSpec((None, slab, side_cols), lambda j, i: (layer, j * m_tiles + i, 0)))
        args.append(side)
        n_side_tiles = side_cols // side_tn
        out_specs = (out_specs if isinstance(out_specs, list) else [out_specs]) + [
            pl.BlockSpec((n_side_tiles, slab, side_tn), lambda j, i: (0, j * m_tiles + i, 0))]
        out_shape = (out_shape if isinstance(out_shape, list) else [out_shape]) + [
            jax.ShapeDtypeStruct((n_side_tiles, side_rows, side_tn), BF16)]
    n_steps = (width // tn) * m_tiles
    if ring:
        scratch += [pltpu.VMEM((RING_SLOTS, tm, D_MODEL), BF16), pltpu.SemaphoreType.DMA((RING_SLOTS,))]
    return pl.pallas_call(
        functools.partial(_inproj_kernel, mode=mode, n_seg=n_seg, has_init=has_init,
                          tiles_per_seq=tiles_per_seq, sub=256, side_cast=side is not None,
                          ring=ring, tm=tm, m_tiles=m_tiles, n_steps=n_steps),
        grid=(width // tn, m // tm),
        in_specs=in_specs,
        out_specs=out_specs,
        out_shape=out_shape,
        scratch_shapes=scratch,
        compiler_params=pltpu.CompilerParams(
            dimension_semantics=("arbitrary", "arbitrary"), vmem_limit_bytes=VMEM_LIMIT),
        name="in_proj_" + mode,
    )(*args)


def _outproj_kernel(m_ref, w_hbm, x_ref, g_ref, lng_ref, lnb_ref, o_ref, acc_ref, wbuf, wsem, *, nj, tn, sub,
                    n_steps):
    j = pl.program_id(1)
    n_mod = g_ref.shape[0]
    tm = m_ref.shape[0]
    rows_per_mod = tm // n_mod
    w_ref = _ring_fetch(lambda k: w_hbm.at[k % nj], wbuf, wsem, pl.program_id(0) * nj + j, n_steps)
    o = jnp.dot(m_ref[...], w_ref[...], preferred_element_type=F32)
    for r in range(n_mod):
        rs = slice(r * rows_per_mod, (r + 1) * rows_per_mod)
        acc_ref[j, rs, :] = ALPHA * x_ref[rs, :] + g_ref[r:r + 1, :] * o[rs, :]

    @pl.when(j == nj - 1)
    def _():
        inv_d = 1.0 / (nj * tn)
        for s0 in range(0, tm, sub):
            rs = slice(s0, s0 + sub)
            total = acc_ref[0, rs, :].sum(axis=-1, keepdims=True)
            for jj in range(1, nj):
                total = total + acc_ref[jj, rs, :].sum(axis=-1, keepdims=True)
            mu = total * inv_d
            sq = jnp.zeros_like(mu)
            for jj in range(nj):
                d = acc_ref[jj, rs, :] - mu
                sq = sq + (d * d).sum(axis=-1, keepdims=True)
            inv = lax.rsqrt(sq * inv_d + LN_EPS)
            for jj in range(nj):
                cs = slice(jj * tn, (jj + 1) * tn)
                o_ref[rs, cs] = (acc_ref[jj, rs, :] - mu) * inv * lng_ref[:, cs] + lnb_ref[:, cs]


def _outproj(mixed, w_out_b, x2d, gate3, ln_g, ln_b, layer, tm, tn):
    m = x2d.shape[0]
    n_mod = gate3.shape[1]
    tiles_per_mod_block = (m // tm) // gate3.shape[0]
    nj = D_MODEL // tn
    vec_spec = pl.BlockSpec((None, 1, D_MODEL), lambda i, j: (layer, 0, 0))
    return pl.pallas_call(
        functools.partial(_outproj_kernel, nj=nj, tn=tn, sub=min(128, tm), n_steps=(m // tm) * nj),
        grid=(m // tm, nj),
        in_specs=[
            pl.BlockSpec((tm, D_INNER), lambda i, j: (i, 0)),
            pl.BlockSpec(memory_space=pl.ANY),
            pl.BlockSpec((tm, tn), lambda i, j: (i, j)),
            pl.BlockSpec((None, n_mod, tn), lambda i, j: (i // tiles_per_mod_block, 0, j)),
            vec_spec,
            vec_spec,
        ],
        out_specs=pl.BlockSpec((tm, D_MODEL), lambda i, j: (i, 0)),
        out_shape=jax.ShapeDtypeStruct((m, D_MODEL), F32),
        scratch_shapes=[pltpu.VMEM((nj, tm, tn), F32), pltpu.VMEM((RING_SLOTS, D_INNER, tn), BF16),
                        pltpu.SemaphoreType.DMA((RING_SLOTS,))],
        compiler_params=pltpu.CompilerParams(
            dimension_semantics=("arbitrary", "arbitrary"), vmem_limit_bytes=VMEM_LIMIT),
        name="out_proj",
    )(mixed, w_out_b, x2d, gate3, ln_g.reshape(DEPTH, 1, D_MODEL), ln_b.reshape(DEPTH, 1, D_MODEL))


def _split3(a):
    hi = a.astype(BF16)
    r1 = a - hi.astype(F32)
    mid = r1.astype(BF16)
    lo = (r1 - mid.astype(F32)).astype(BF16)
    return hi, mid, lo


def _mixer_kernel(*refs, tv, nchunks, start_pos, has_init, has_stack, layer):
    if has_init:
        (u_ref, gz_ref, xc_ref, dtr_ref, pool0_ref, ssm0_ref, *rest) = refs
    else:
        (u_ref, gz_ref, xc_ref, dtr_ref, *rest) = refs
    (wpool_ref, pscale_ref, dtb_ref, alog_ref, dskip_ref, normw_ref, *rest) = rest
    if has_stack:
        rest = rest[1:]
    mixed_ref, npool_ref, nssm_ref, poolbuf, winbuf, h_ref = rest
    t = MIX_T
    c = pl.program_id(1)

    def pad_rows(v):
        if tv == t:
            return v
        return jnp.concatenate([v, jnp.zeros((t - tv, v.shape[1]), v.dtype)], axis=0)

    @pl.when(c == 0)
    def _init():
        if has_init:
            poolbuf[0:POOL_BUF, :] = pool0_ref[...]
            for g in range(GROUPS):
                for j in range(PAIRS_PER_GROUP):
                    r0 = (g * HPG + 2 * j) * HEAD_DIM
                    h_ref[g, :, j * LANES:(j + 1) * LANES] = ssm0_ref[r0:r0 + LANES, :].T
        else:
            poolbuf[0:POOL_BUF, :] = jnp.zeros((POOL_BUF, W_POOL), F32)
            h_ref[...] = jnp.zeros(h_ref.shape, F32)

    poolbuf[POOL_BUF:POOL_BUF + tv, :] = u_ref[...]
    row = lax.broadcasted_iota(jnp.int32, (tv, 1), 0)
    pos = start_pos + c * tv + row
    for g, w in enumerate(POOL_WINDOWS):
        cols = slice(g * POOL_GW, (g + 1) * POOL_GW)
        u_g = poolbuf[POOL_BUF:POOL_BUF + tv, cols]
        src, d, level = poolbuf, 1, 0
        while 2 * d < w:
            lo = POOL_BUF - (w - 2 * d)
            dst = winbuf.at[level % 2]
            dst[lo:POOL_BUF + tv, cols] = src[lo:POOL_BUF + tv, cols] + src[lo - d:POOL_BUF + tv - d, cols]
            src, d, level = dst, 2 * d, level + 1
        s = src[POOL_BUF:POOL_BUF + tv, cols] + src[POOL_BUF - d:POOL_BUF + tv - d, cols]
        cnt = jnp.minimum(pos + 1, w).astype(F32)
        pooled = (s / cnt - u_g).astype(BF16)
        po = jnp.dot(pooled, wpool_ref[g], preferred_element_type=F32)
        mixed_ref[:, cols] = (po * pscale_ref[:, cols] * gz_ref[:, cols]).astype(BF16)
    poolbuf[0:POOL_BUF, :] = poolbuf[tv:tv + POOL_BUF, :]

    x_dt = dtr_ref[...] + dtb_ref[...]
    dt = jnp.maximum(x_dt, 0.0) + jnp.log1p(jnp.exp(-jnp.abs(x_dt)))
    if tv < t:
        dt = jnp.concatenate([dt, jnp.zeros((t - tv, LANES), F32)], axis=0)
    a = dt * (-LOG2E * jnp.exp(alog_ref[...]))
    ri = lax.broadcasted_iota(jnp.int32, (t, t), 0)
    ci = lax.broadcasted_iota(jnp.int32, (t, t), 1)
    tril = ri >= ci
    tri = jnp.where(tril, 1.0, 0.0).astype(BF16)
    a_cs = sum(jnp.dot(tri, piece, preferred_element_type=F32) for piece in _split3(a))
    a_last = a_cs[t - 1:t, :]
    ea_last = jnp.exp2(a_last)
    a_cs_t = a_cs.T
    dt_t = dt.T
    w_t = (jnp.exp2(a_last - a_cs) * dt).T
    lo_half = lax.broadcasted_iota(jnp.int32, (1, LANES), 1) < HEAD_DIM

    def ssd_group(g):
        b_g = pad_rows(xc_ref[:, W_SSD + g * STATE:W_SSD + (g + 1) * STATE])
        c_g = pad_rows(xc_ref[:, W_SSD + GROUPS * STATE + g * STATE:W_SSD + GROUPS * STATE + (g + 1) * STATE])
        cb = lax.dot_general(c_g.astype(BF16), b_g.astype(BF16), (((1,), (1,)), ((), ())),
                             preferred_element_type=F32)
        b_t = b_g.T
        ys = []
        for j in range(PAIRS_PER_GROUP):
            p = g * PAIRS_PER_GROUP + j
            pc = slice(p * LANES, (p + 1) * LANES)
            jc = slice(j * LANES, (j + 1) * LANES)
            xs_p = pad_rows(xc_ref[:, pc])
            h_p = h_ref[g, :, jc]
            x_b = xs_p.astype(BF16)
            rhs = jnp.concatenate([x_b, h_p.astype(BF16)], axis=0)
            me_parts, b_parts = [], []
            for k in (2 * p, 2 * p + 1):
                col = jnp.broadcast_to(a_cs[0:tv, k:k + 1], (tv, t))
                lmat = jnp.where(tril[0:tv], jnp.exp2(col - a_cs_t[k:k + 1, :]), 0.0)
                me_parts.append(jnp.concatenate(
                    [(cb[0:tv] * lmat * dt_t[k:k + 1, :]).astype(BF16),
                     (c_g[0:tv] * jnp.exp2(col)).astype(BF16)], axis=1))
                b_parts.append((b_t * w_t[k:k + 1, :]).astype(BF16))
            y_both = jnp.dot(jnp.concatenate(me_parts, axis=0), rhs, preferred_element_type=F32)
            y_mm = jnp.where(lo_half, y_both[0:tv], y_both[tv:2 * tv])
            ys.append(y_mm + dskip_ref[:, pc] * xs_p[0:tv])
            st_both = jnp.dot(jnp.concatenate(b_parts, axis=0), x_b, preferred_element_type=F32)
            st = jnp.where(lo_half, st_both[0:STATE], st_both[STATE:2 * STATE])
            cd = jnp.where(lo_half, ea_last[:, 2 * p:2 * p + 1], ea_last[:, 2 * p + 1:2 * p + 2])
            h_ref[g, :, jc] = cd * h_p + st
        gc = slice(g * GROUP_W, (g + 1) * GROUP_W)
        y_g = jnp.concatenate(ys, axis=1)
        y_g = y_g * gz_ref[:, W_POOL + g * GROUP_W:W_POOL + (g + 1) * GROUP_W]
        ms = jnp.mean(y_g * y_g, axis=-1, keepdims=True)
        out = y_g * lax.rsqrt(ms + RMS_EPS) * normw_ref[:, gc]
        mixed_ref[:, W_POOL + g * GROUP_W:W_POOL + (g + 1) * GROUP_W] = out.astype(BF16)

    for g in range(GROUPS):
        ssd_group(g)

    @pl.when(c == nchunks - 1)
    def _fin():
        npool_ref[...] = poolbuf[0:POOL_BUF, :]
        if has_stack:
            state_out = nssm_ref
        else:
            state_out = nssm_ref.at[layer]
            for other in range(DEPTH):
                if other != layer:
                    nssm_ref[other] = jnp.zeros((W_SSD, STATE), F32)
        for g in range(GROUPS):
            for j in range(PAIRS_PER_GROUP):
                r0 = (g * HPG + 2 * j) * HEAD_DIM
                state_out[r0:r0 + LANES, :] = h_ref[g, :, j * LANES:(j + 1) * LANES].T


def _mixer(u, gz, xc, dt_raw, init, ssm_stack, params, layer, nb, seq, start_pos):
    tv = min(seq, MIX_T)
    nchunks = seq // tv
    has_init = init is not None
    has_stack = ssm_stack is not None
    row_map = lambda b, c: (b * nchunks + c, 0)
    per_layer2 = lambda b, c: (layer, 0, 0)
    per_b = lambda b, c: (b, 0, 0)
    per_layer_b = lambda b, c: (layer, b, 0, 0)
    in_specs = [pl.BlockSpec((tv, W_POOL), row_map), pl.BlockSpec((tv, W_POOL + W_SSD), row_map),
                pl.BlockSpec((tv, CONV_DIM), row_map), pl.BlockSpec((tv, LANES), row_map)]
    args = [u, gz, xc, dt_raw]
    if has_init:
        in_specs += [pl.BlockSpec((None, None, POOL_BUF, W_POOL), per_layer_b),
                     pl.BlockSpec((None, None, W_SSD, STATE), per_layer_b)]
        args += list(init)
    in_specs += [
        pl.BlockSpec((None, len(POOL_WINDOWS), POOL_GW, POOL_GW), lambda b, c: (layer, 0, 0, 0)),
        pl.BlockSpec((None, 1, W_POOL), per_layer2),
        pl.BlockSpec((None, 1, LANES), per_layer2),
        pl.BlockSpec((None, 1, LANES), per_layer2),
        pl.BlockSpec((None, 1, W_SSD), per_layer2),
        pl.BlockSpec((None, 1, W_SSD), per_layer2),
    ]
    args += list(params)
    aliases = {}
    if has_stack:
        in_specs.append(pl.BlockSpec(memory_space=pl.ANY))
        args.append(ssm_stack)
        aliases = {len(args) - 1: 2}
    return pl.pallas_call(
        functools.partial(_mixer_kernel, tv=tv, nchunks=nchunks, start_pos=start_pos, has_init=has_init,
                          has_stack=has_stack, layer=layer),
        grid=(nb, nchunks),
        in_specs=in_specs,
        out_specs=[
            pl.BlockSpec((tv, D_INNER), row_map),
            pl.BlockSpec((None, POOL_BUF, W_POOL), per_b),
            (pl.BlockSpec((None, None, W_SSD, STATE), per_layer_b) if has_stack
             else pl.BlockSpec((DEPTH, None, W_SSD, STATE), lambda b, c: (0, b, 0, 0))),
        ],
        out_shape=[
            jax.ShapeDtypeStruct((nb * seq, D_INNER), BF16),
            jax.ShapeDtypeStruct((nb, POOL_BUF, W_POOL), F32),
            jax.ShapeDtypeStruct((DEPTH, nb, W_SSD, STATE), F32),
        ],
        input_output_aliases=aliases,
        scratch_shapes=[
            pltpu.VMEM((POOL_BUF + tv, W_POOL), F32),
            pltpu.VMEM((2, POOL_BUF + tv, W_POOL), F32),
            pltpu.VMEM((GROUPS, STATE, GROUP_W), F32),
        ],
        compiler_params=pltpu.CompilerParams(
            dimension_semantics=("parallel", "arbitrary"), vmem_limit_bytes=VMEM_LIMIT),
        name="mixer_init" if has_init else "mixer_zero",
    )(*args)


def _pad_rows_front(a, rows):
    pad = rows - a.shape[-2]
    return jnp.pad(a, [(0, 0)] * (a.ndim - 2) + [(pad, 0), (0, 0)])


def _pad_lanes(a):
    return jnp.pad(a, [(0, 0)] * (a.ndim - 1) + [(0, LANES - a.shape[-1])])


def kernel(x_prompt, x_sample, state_pool, state_conv, state_ssm, c_prompt, c_sample,
           w_ada, b_ada, w_in, w_pool, pool_scale, conv_w, conv_b,
           dt_bias, a_log, d_skip, ssd_norm_w, w_out, ln_g, ln_b):
    nbp, seq_p, _ = x_prompt.shape
    nbs, seq_s, _ = x_sample.shape
    past_len = 1024

    n_c = nbp + nbs
    c_rows = -(-n_c // 8) * 8
    c_all = jnp.concatenate([c_prompt, c_sample, jnp.zeros((c_rows - n_c, D_MODEL), F32)], axis=0)
    mod = _ada(c_all, w_ada, b_ada)

    w_in_t = jnp.swapaxes(w_in, 1, 2)
    w_dt_b = _pad_lanes(w_in[:, :, MAIN_DIM:]).astype(BF16)
    conv_b3 = conv_b[:, None]
    params = (w_pool.astype(BF16), pool_scale[:, None],
              _pad_lanes(dt_bias)[:, None], _pad_lanes(a_log)[:, None],
              jnp.repeat(d_skip, HEAD_DIM, axis=-1)[:, None], ssd_norm_w[:, None])
    conv0 = _pad_rows_front(state_conv, CONV_BUF)
    init = (_pad_rows_front(state_pool, POOL_BUF),
            state_ssm.reshape(DEPTH, nbs, W_SSD, STATE))

    def in_projections(h, l, tm, hist, n_seg, tiles_per_seq, side=None):
        many = h.shape[0] > tm
        plain = (dict(tm=IN_TILE_MANY[0], tn=IN_TILE_MANY[1], ring=True) if many
                 else dict(tm=h.shape[0], tn=IN_TILE_ONE))
        u = _inproj(h, w_in_t, l, OFF_U, W_POOL, "raw", **plain)
        gz = _inproj(h, w_in_t, l, OFF_G, W_POOL + W_SSD, "silu", **plain)
        xc, nconv, *side_b = _inproj(h, w_in_t, l, OFF_XBC, CONV_DIM, "conv", tm, CONV_TN, side=side,
                                     side_tn=OUT_TN, conv=(conv_w, conv_b3, hist), n_seg=n_seg,
                                     tiles_per_seq=tiles_per_seq, ring=many)
        return (u, gz, xc, nconv, *side_b)

    xp = x_prompt.reshape(nbp * seq_p, D_MODEL)
    xs = x_sample.reshape(nbs * seq_s, D_MODEL)
    outs = {k: [] for k in ("pp", "pc", "sp", "sc")}
    tm_p = IN_TILE_MANY[0]
    tm_s = nbs * seq_s
    ssm_p = ssm_s = None
    for l in range(DEPTH):
        shift, scale, gate = (mod[l, :, i * D_MODEL:(i + 1) * D_MODEL] for i in range(3))

        sc3, sh3, g3 = (v[0:nbp].reshape(nbp, 1, D_MODEL) for v in (scale, shift, gate))
        h, dt_raw = _lnmod(xp, sc3, sh3, w_dt_b, l, tm=ROW_TILE)
        u, gz, xc, nconv, w_out_b = in_projections(h, l, tm_p, None, 1, seq_p // tm_p, side=w_out)
        mixed, npool, ssm_p = _mixer(u, gz, xc, dt_raw, None, ssm_p, params, l, nbp, seq_p, 0)
        xp = _outproj(mixed, w_out_b, xp, g3, ln_g, ln_b, l, tm=ROW_TILE, tn=OUT_TN)
        outs["pp"].append(npool[:, 1:])
        outs["pc"].append(nconv[:, CONV_BUF - (CONV_W - 1):])

        sc3, sh3, g3 = (v[nbp:n_c].reshape(1, nbs, D_MODEL) for v in (scale, shift, gate))
        h, dt_raw = _lnmod(xs, sc3, sh3, w_dt_b, l, tm=tm_s)
        u, gz, xc, nconv = in_projections(h, l, tm_s, conv0, nbs, 1)
        mixed, npool, ssm_s = _mixer(u, gz, xc, dt_raw, init, ssm_s, params, l, nbs, seq_s, past_len)
        xs = _outproj(mixed, w_out_b, xs, g3, ln_g, ln_b, l, tm=tm_s, tn=OUT_TN)
        outs["sp"].append(npool[:, 1:])
        outs["sc"].append(nconv[:, CONV_BUF - (CONV_W - 1):])

    return (xp.reshape(nbp, seq_p, D_MODEL), xs.reshape(nbs, seq_s, D_MODEL),
            jnp.stack(outs["pp"]), jnp.stack(outs["pc"]), ssm_p.reshape(DEPTH, nbp, HEADS, HEAD_DIM, STATE),
            jnp.stack(outs["sp"]), jnp.stack(outs["sc"]), ssm_s.reshape(DEPTH, nbs, HEADS, HEAD_DIM, STATE))
```
